```python
import jax, jax.numpy as jnp
from jax import lax
import numpy as np

D_MODEL = 4096
BATCH = 4
SEQ = 2048
DEPTH = 2
DEC_BATCH = 32
DEC_SEQ = 1
PAST_LEN = 16384
PAGE_SIZE = 128

N_A = DEPTH // 2
N_B = DEPTH - N_A
RW_HEAD = 64
RW_HEADS = D_MODEL // RW_HEAD
D_DECAY_LORA = max(32, int(round(1.8 * D_MODEL ** 0.5 / 32)) * 32)
D_AAA_LORA = max(32, int(round(1.8 * D_MODEL ** 0.5 / 32)) * 32)
D_GATE_LORA = max(32, int(round(0.6 * D_MODEL ** 0.8 / 32)) * 32)
GN_EPS = 64e-5
N_SHIFT_MIX = 6
HEAD_DIM = 64
N_Q_HEADS = D_MODEL // HEAD_DIM
N_KV_HEADS = max(1, N_Q_HEADS // 8)
GQA_GROUP = N_Q_HEADS // N_KV_HEADS
WINDOW = 128
BLOCK = WINDOW
ATTN_SCALE = HEAD_DIM ** -0.5
D_FF = ((7 * D_MODEL // 2 + 127) // 128) * 128
CONV_W = 3
RMS_EPS = 1e-6

kernel_name = 'yoco_rwkv7_swa_sink_convffn_step'


def _rmsnorm(x, g):
    x32 = x.astype(jnp.float32)
    y = x32 * lax.rsqrt(jnp.mean(x32 * x32, axis=-1, keepdims=True) + RMS_EPS)
    return (y * g.astype(jnp.float32)).astype(x.dtype)


def _adaln(c, w, b, n):
    m = jax.nn.silu(c) @ w + b
    return jnp.split(m[:, None, :], n, axis=-1)


def _rwkv7_mix(h, shift_prev, S0, mix, w0, w1, w2, a0, a1, a2, g1, g2, k_k, k_a, r_k,
               w_r, w_k, w_v, w_o, lnx_w, lnx_b):
    B, T, D = h.shape
    H, N = RW_HEADS, RW_HEAD
    f32 = jnp.float32
    h_prev = jnp.concatenate([shift_prev[:, None, :].astype(h.dtype), h[:, :-1]], axis=1)
    dx = h_prev - h
    xr, xw, xk, xv, xa, xg = [h + dx * mix[i] for i in range(N_SHIFT_MIX)]
    r = (xr @ w_r).astype(f32)
    k = (xk @ w_k).astype(f32)
    v = (xv @ w_v).astype(f32)
    w_log = -jax.nn.softplus(-(w0 + jnp.tanh(xw @ w1) @ w2).astype(f32)) - 0.5
    decay = jnp.exp(-jnp.exp(w_log))
    a = jax.nn.sigmoid((a0 + (xa @ a1) @ a2).astype(f32))
    g = jax.nn.sigmoid(xg @ g1) @ g2
    kk = (k * k_k.astype(f32)).reshape(B, T, H, N)
    kk = kk / jnp.maximum(jnp.sqrt(jnp.sum(kk * kk, axis=-1, keepdims=True)), 1e-12)
    k = k * (1.0 + (a - 1.0) * k_a.astype(f32))
    r4, k4, v4 = r.reshape(B, T, H, N), k.reshape(B, T, H, N), v.reshape(B, T, H, N)
    d4, a4 = decay.reshape(B, T, H, N), a.reshape(B, T, H, N)
    tm = lambda t: jnp.swapaxes(t, 0, 1)
    xs = (tm(r4), tm(k4), tm(v4), tm(d4), tm(kk), tm(kk * a4))

    def step(S, inp):
        r_t, k_t, v_t, d_t, kk_t, b_t = inp
        s_kk = jnp.einsum('bhvk,bhk->bhv', S, kk_t)
        S = (S * d_t[:, :, None, :] - s_kk[..., None] * b_t[:, :, None, :]
             + v_t[..., None] * k_t[:, :, None, :])
        y_t = jnp.einsum('bhvk,bhk->bhv', S, r_t)
        return S, y_t

    S_T, y = lax.scan(step, S0.astype(f32), xs)
    y = jnp.swapaxes(y, 0, 1)
    mu = jnp.mean(y, axis=-1, keepdims=True)
    var = jnp.mean(jnp.square(y - mu), axis=-1, keepdims=True)
    y = (y - mu) * lax.rsqrt(var + GN_EPS) * lnx_w.astype(f32).reshape(H, N) + lnx_b.astype(f32).reshape(H, N)
    y = y + jnp.sum(r4 * k4 * r_k.astype(f32), axis=-1, keepdims=True) * v4
    out = (y.reshape(B, T, D).astype(h.dtype) * g) @ w_o
    return out, S_T.astype(S0.dtype), h[:, -1]


def _conv_ffn(h, buf, w_in, conv_w, conv_b, w_out):
    T = h.shape[1]
    gate, up = jnp.split(h @ w_in, 2, axis=-1)
    pad = jnp.concatenate([buf.astype(gate.dtype), gate], axis=1)
    conv = conv_b + sum(pad[:, j:j + T] * conv_w[j] for j in range(CONV_W))
    out = (jax.nn.silu(conv) * up) @ w_out
    return out, pad[:, -(CONV_W - 1):]


def _shared_kv(x, c, kv_norm_g, kv_mod_w, kv_mod_b, w_kv, k_norm_g):
    B, T, _ = x.shape
    sh, sc = _adaln(c, kv_mod_w, kv_mod_b, 2)
    hn = _rmsnorm(x, kv_norm_g) * (1 + sc) + sh
    k, v = jnp.split(hn @ w_kv, 2, axis=-1)
    k = _rmsnorm(k.reshape(B, T, N_KV_HEADS, HEAD_DIM), k_norm_g)
    v = v.reshape(B, T, N_KV_HEADS, HEAD_DIM)
    return k, v


def _sink_attend(q, k, v, mask, sinks):
    s = jnp.einsum('...qkgd,...skd->...kgqs', q, k, preferred_element_type=jnp.float32) * ATTN_SCALE
    s = jnp.where(mask, s, -jnp.inf)
    sink = sinks.astype(jnp.float32).reshape(N_KV_HEADS, GQA_GROUP, 1, 1)
    m = jnp.maximum(jnp.max(s, axis=-1, keepdims=True), sink)
    p = jnp.exp(s - m)
    p = p / (jnp.sum(p, axis=-1, keepdims=True) + jnp.exp(sink - m))
    return jnp.einsum('...kgqs,...skd->...qkgd', p.astype(v.dtype), v)


def _swa_banded(q, k, v, sinks):
    B, T = q.shape[:2]
    nb = T // BLOCK
    qb = q.reshape(B, nb, BLOCK, N_KV_HEADS, GQA_GROUP, HEAD_DIM)
    kb = k.reshape(B, nb, BLOCK, N_KV_HEADS, HEAD_DIM)
    vb = v.reshape(B, nb, BLOCK, N_KV_HEADS, HEAD_DIM)
    prev = lambda t: jnp.concatenate([jnp.zeros_like(t[:, :1]), t[:, :-1]], axis=1)
    k_band = jnp.concatenate([prev(kb), kb], axis=2)
    v_band = jnp.concatenate([prev(vb), vb], axis=2)
    blk = jnp.arange(nb)[:, None]
    qpos = blk * BLOCK + jnp.arange(BLOCK)[None]
    kpos = (blk - 1) * BLOCK + jnp.arange(2 * BLOCK)[None]
    rel = qpos[:, :, None] - kpos[:, None, :]
    mask = (rel >= 0) & (rel < WINDOW) & (kpos[:, None, :] >= 0)
    o = _sink_attend(qb, k_band, v_band, mask[:, None, None], sinks)
    return o.reshape(B, T, N_Q_HEADS * HEAD_DIM)


def _swa_cached(q, k_all, v_all, n_past, sinks):
    B, T = q.shape[:2]
    krel = jnp.arange(n_past + T) - n_past
    rel = jnp.arange(T)[:, None] - krel[None, :]
    mask = (rel >= 0) & (rel < WINDOW)
    o = _sink_attend(q, k_all, v_all, mask, sinks)
    return o.reshape(B, T, N_Q_HEADS * HEAD_DIM)


def _forward(x, c, wkv0, shift0, conv0, k_buf, v_buf, p):
    B, T, _ = x.shape
    new_wkv, new_shift, new_conv = [], [], []
    k_att = v_att = k_win = v_win = None
    for l in range(DEPTH):
        sh1, sc1, gt1, sh2, sc2, gt2 = _adaln(c, p['mod_w'][l], p['mod_b'][l], 6)
        h = _rmsnorm(x, p['ln1_g'][l]) * (1 + sc1) + sh1
        if l < N_A:
            o, S, s = _rwkv7_mix(h, shift0[l], wkv0[l], p['rwkv_mix'][l], p['rwkv_w0'][l], p['rwkv_w1'][l],
                                 p['rwkv_w2'][l], p['rwkv_a0'][l], p['rwkv_a1'][l], p['rwkv_a2'][l],
                                 p['rwkv_g1'][l], p['rwkv_g2'][l], p['rwkv_k_k'][l], p['rwkv_k_a'][l],
                                 p['rwkv_r_k'][l], p['rwkv_w_r'][l], p['rwkv_w_k'][l], p['rwkv_w_v'][l],
                                 p['rwkv_w_o'][l], p['rwkv_lnx_w'][l], p['rwkv_lnx_b'][l])
            new_wkv.append(S)
            new_shift.append(s)
        else:
            j = l - N_A
            q = (h @ p['attn_w_q'][j]).reshape(B, T, N_KV_HEADS, GQA_GROUP, HEAD_DIM)
            q = _rmsnorm(q, p['attn_q_norm_g'][j])
            if k_buf is None:
                attn = _swa_banded(q, k_att, v_att, p['attn_sinks'][j])
            else:
                attn = _swa_cached(q, k_att, v_att, k_buf.shape[1], p['attn_sinks'][j])
            o = attn @ p['attn_w_o'][j]
        x = x + gt1 * o
        h2 = _rmsnorm(x, p['ln2_g'][l]) * (1 + sc2) + sh2
        f, cb = _conv_ffn(h2, conv0[l], p['ffn_w_in'][l], p['ffn_conv_w'][l], p['ffn_conv_b'][l], p['ffn_w_out'][l])
        new_conv.append(cb)
        x = x + gt2 * f
        if l == N_A - 1:
            k_new, v_new = _shared_kv(x, c, p['kv_norm_g'], p['kv_mod_w'], p['kv_mod_b'], p['w_kv'], p['k_norm_g'])
            if k_buf is None:
                w = min(WINDOW, T)
                k_att, v_att = k_new, v_new
                k_win, v_win = k_new[:, T - w:], v_new[:, T - w:]
            else:
                n_buf = k_buf.shape[1]
                k_att = jnp.concatenate([k_buf.astype(k_new.dtype), k_new], axis=1)
                v_att = jnp.concatenate([v_buf.astype(v_new.dtype), v_new], axis=1)
                k_win, v_win = k_att[:, -n_buf:], v_att[:, -n_buf:]
    return x, jnp.stack(new_wkv), jnp.stack(new_shift), jnp.stack(new_conv), k_win, v_win


def setup_inputs(seed: int = 0) -> dict:
    key = jax.random.key(seed)
    ks = iter(jax.random.split(key, 64))
    f32 = jnp.float32
    nrm = lambda shape, scale: scale * jax.random.normal(next(ks), shape, f32)
    uni = lambda shape, lo, hi: jax.random.uniform(next(ks), shape, f32, lo, hi)
    D, H, N = D_MODEL, RW_HEADS, RW_HEAD
    inv = D ** -0.5
    W_BUF = min(WINDOW, PAST_LEN)
    QW = N_Q_HEADS * HEAD_DIM
    KVW = N_KV_HEADS * HEAD_DIM
    return dict(
        x_prompt=nrm((BATCH, SEQ, D), 1.0),
        x_sample=nrm((DEC_BATCH, DEC_SEQ, D), 1.0),
        c_prompt=nrm((BATCH, D), 1.0),
        c_sample=nrm((DEC_BATCH, D), 1.0),
        state_wkv=nrm((N_A, DEC_BATCH, H, N, N), 0.5),
        state_shift=nrm((N_A, DEC_BATCH, D), 1.0),
        state_conv=nrm((DEPTH, DEC_BATCH, CONV_W - 1, D_FF), 1.0),
        cache_k_win=nrm((DEC_BATCH, W_BUF, N_KV_HEADS, HEAD_DIM), 1.0),
        cache_v_win=nrm((DEC_BATCH, W_BUF, N_KV_HEADS, HEAD_DIM), 1.0),
        mod_w=nrm((DEPTH, D, 6 * D), 0.5 * inv),
        mod_b=nrm((DEPTH, 6 * D), 0.02),
        ln1_g=1.0 + nrm((DEPTH, D), 0.02),
        ln2_g=1.0 + nrm((DEPTH, D), 0.02),
        rwkv_mix=uni((N_A, N_SHIFT_MIX, D), 0.0, 1.0),
        rwkv_w0=uni((N_A, D), -6.0, 1.0),
        rwkv_w1=nrm((N_A, D, D_DECAY_LORA), inv),
        rwkv_w2=nrm((N_A, D_DECAY_LORA, D), 0.5 * D_DECAY_LORA ** -0.5),
        rwkv_a0=nrm((N_A, D), 0.1),
        rwkv_a1=nrm((N_A, D, D_AAA_LORA), inv),
        rwkv_a2=nrm((N_A, D_AAA_LORA, D), 0.5 * D_AAA_LORA ** -0.5),
        rwkv_g1=nrm((N_A, D, D_GATE_LORA), inv),
        rwkv_g2=nrm((N_A, D_GATE_LORA, D), D_GATE_LORA ** -0.5),
        rwkv_k_k=0.85 + nrm((N_A, D), 0.02),
        rwkv_k_a=1.0 + nrm((N_A, D), 0.02),
        rwkv_r_k=nrm((N_A, H, N), 0.1),
        rwkv_w_r=nrm((N_A, D, D), inv),
        rwkv_w_k=nrm((N_A, D, D), inv),
        rwkv_w_v=nrm((N_A, D, D), inv),
        rwkv_w_o=nrm((N_A, D, D), inv),
        rwkv_lnx_w=1.0 + nrm((N_A, D), 0.02),
        rwkv_lnx_b=nrm((N_A, D), 0.02),
        kv_norm_g=1.0 + nrm((D,), 0.02),
        kv_mod_w=nrm((D, 2 * D), 0.5 * inv),
        kv_mod_b=nrm((2 * D,), 0.02),
        w_kv=nrm((D, 2 * KVW), inv),
        k_norm_g=1.0 + nrm((HEAD_DIM,), 0.02),
        attn_w_q=nrm((N_B, D, QW), inv),
        attn_q_norm_g=1.0 + nrm((N_B, HEAD_DIM), 0.02),
        attn_sinks=nrm((N_B, N_Q_HEADS), 0.5),
        attn_w_o=nrm((N_B, QW, D), QW ** -0.5),
        ffn_w_in=nrm((DEPTH, D, 2 * D_FF), inv),
        ffn_conv_w=nrm((DEPTH, CONV_W, D_FF), CONV_W ** -0.5),
        ffn_conv_b=nrm((DEPTH, D_FF), 0.02),
        ffn_w_out=nrm((DEPTH, D_FF, D), D_FF ** -0.5),
    )


def reference(x_prompt, x_sample, c_prompt, c_sample, state_wkv, state_shift, state_conv,
              cache_k_win, cache_v_win, mod_w, mod_b, ln1_g, ln2_g,
              rwkv_mix, rwkv_w0, rwkv_w1, rwkv_w2, rwkv_a0, rwkv_a1, rwkv_a2, rwkv_g1, rwkv_g2,
              rwkv_k_k, rwkv_k_a, rwkv_r_k, rwkv_w_r, rwkv_w_k, rwkv_w_v, rwkv_w_o,
              rwkv_lnx_w, rwkv_lnx_b, kv_norm_g, kv_mod_w, kv_mod_b, w_kv, k_norm_g,
              attn_w_q, attn_q_norm_g, attn_sinks, attn_w_o,
              ffn_w_in, ffn_conv_w, ffn_conv_b, ffn_w_out):
    p = dict(mod_w=mod_w, mod_b=mod_b, ln1_g=ln1_g, ln2_g=ln2_g,
             rwkv_mix=rwkv_mix, rwkv_w0=rwkv_w0, rwkv_w1=rwkv_w1, rwkv_w2=rwkv_w2,
             rwkv_a0=rwkv_a0, rwkv_a1=rwkv_a1, rwkv_a2=rwkv_a2, rwkv_g1=rwkv_g1, rwkv_g2=rwkv_g2,
             rwkv_k_k=rwkv_k_k, rwkv_k_a=rwkv_k_a, rwkv_r_k=rwkv_r_k, rwkv_w_r=rwkv_w_r,
             rwkv_w_k=rwkv_w_k, rwkv_w_v=rwkv_w_v, rwkv_w_o=rwkv_w_o,
             rwkv_lnx_w=rwkv_lnx_w, rwkv_lnx_b=rwkv_lnx_b,
             kv_norm_g=kv_norm_g, kv_mod_w=kv_mod_w, kv_mod_b=kv_mod_b, w_kv=w_kv, k_norm_g=k_norm_g,
             attn_w_q=attn_w_q, attn_q_norm_g=attn_q_norm_g, attn_sinks=attn_sinks, attn_w_o=attn_w_o,
             ffn_w_in=ffn_w_in, ffn_conv_w=ffn_conv_w, ffn_conv_b=ffn_conv_b, ffn_w_out=ffn_w_out)
    B = x_prompt.shape[0]
    dt = x_prompt.dtype
    wkv0 = jnp.zeros((N_A, B, RW_HEADS, RW_HEAD, RW_HEAD), dt)
    shift0 = jnp.zeros((N_A, B, D_MODEL), dt)
    conv0 = jnp.zeros((DEPTH, B, CONV_W - 1, D_FF), dt)
    y_prompt, wkv_p, shift_p, conv_p, kwin_p, vwin_p = _forward(
        x_prompt, c_prompt, wkv0, shift0, conv0, None, None, p)
    y_sample, wkv_s, shift_s, conv_s, kwin_s, vwin_s = _forward(
        x_sample, c_sample, state_wkv, state_shift, state_conv, cache_k_win, cache_v_win, p)
    return (y_prompt, y_sample, wkv_p, wkv_s, shift_p, shift_s, conv_p, conv_s,
            kwin_p, kwin_s, vwin_p, vwin_s)
```

```python
import functools

import jax
import jax.numpy as jnp
from jax import lax
from jax.experimental import pallas as pl
from jax.experimental.pallas import tpu as pltpu

F32 = jnp.float32
BF16 = jnp.bfloat16

D_MODEL = 4096
RW_HEAD = 64
RW_HEADS = D_MODEL // RW_HEAD
GN_EPS = 64e-5
RMS_EPS = 1e-6
HEAD_DIM = 64
N_Q_HEADS = D_MODEL // HEAD_DIM
N_KV_HEADS = N_Q_HEADS // 8
GQA_GROUP = N_Q_HEADS // N_KV_HEADS
WINDOW = 128
ATTN_SCALE = HEAD_DIM ** -0.5
CONV_W = 3

LANES = 128
VMEM_LIMIT_BYTES = 56 * 1024 * 1024

RW_CHUNK = 64
RW_LANES_PER_STEP = 512


def _cparams(sem):
    return pltpu.CompilerParams(dimension_semantics=sem, vmem_limit_bytes=VMEM_LIMIT_BYTES)


def _mm_body(*refs, nk, has_bias, act, has_res):
    x_ref, w_ref = refs[0], refs[1]
    pos = 2
    bias_ref = res_ref = gate_ref = None
    if has_bias:
        bias_ref = refs[pos]; pos += 1
    if has_res:
        res_ref, gate_ref = refs[pos], refs[pos + 1]; pos += 2
    o_ref = refs[pos]
    acc_ref = refs[pos + 1] if nk > 1 else None

    part = jnp.dot(x_ref[...].astype(BF16), w_ref[...].astype(BF16), preferred_element_type=F32)

    def finish(acc):
        if has_bias:
            acc = acc + bias_ref[...]
        if act == "tanh":
            acc = jnp.tanh(acc)
        elif act == "sigmoid":
            acc = 1.0 / (1.0 + jnp.exp(-acc))
        if has_res:
            acc = res_ref[...] + gate_ref[...] * acc
        o_ref[...] = acc.astype(o_ref.dtype)

    if nk == 1:
        finish(part)
    else:
        k = pl.program_id(2)

        @pl.when(k == 0)
        def _():
            acc_ref[...] = part

        @pl.when(k > 0)
        def _():
            acc_ref[...] += part

        @pl.when(k == nk - 1)
        def _():
            finish(acc_ref[...])


def _pick_tile(dim, prefs):
    for p in prefs:
        if dim >= p and dim % p == 0:
            return p
    return dim


def _mm(x, w, *, bias=None, act=None, res=None, gate=None, rows_per_gate=1,
        out_dtype=F32, name="mm"):
    M, K = x.shape
    K2, N = w.shape
    assert K == K2
    tm = _pick_tile(M, (1024,))
    tn = _pick_tile(N, (1024, 512))
    tk = _pick_tile(K, (2048,))
    nk = K // tk
    grid = (M // tm, N // tn, nk)

    in_specs = [pl.BlockSpec((tm, tk), lambda i, j, k: (i, k)),
                pl.BlockSpec((tk, tn), lambda i, j, k: (k, j))]
    args = [x, w]
    if bias is not None:
        in_specs.append(pl.BlockSpec((1, tn), lambda i, j, k: (0, j)))
        args.append(bias.reshape(1, N).astype(F32))
    if res is not None:
        in_specs.append(pl.BlockSpec((tm, tn), lambda i, j, k: (i, j)))
        args.append(res)
        if rows_per_gate == 1:
            in_specs.append(pl.BlockSpec((tm, tn), lambda i, j, k: (i, j)))
            args.append(gate)
        else:
            assert rows_per_gate % tm == 0
            bpg = rows_per_gate // tm
            in_specs.append(pl.BlockSpec((None, 1, tn), lambda i, j, k: (i // bpg, 0, j)))
            args.append(gate.reshape(gate.shape[0], 1, N))
    scratch = [pltpu.VMEM((tm, tn), F32)] if nk > 1 else []
    body = functools.partial(_mm_body, nk=nk, has_bias=bias is not None, act=act,
                             has_res=res is not None)
    return pl.pallas_call(
        body,
        grid=grid,
        in_specs=in_specs,
        out_specs=pl.BlockSpec((tm, tn), lambda i, j, k: (i, j)),
        out_shape=jax.ShapeDtypeStruct((M, N), out_dtype),
        scratch_shapes=scratch,
        compiler_params=_cparams(("parallel", "parallel", "arbitrary")),
        name=name,
    )(*args)


def _dot(a, b):
    return jnp.dot(a.astype(BF16), b.astype(BF16), preferred_element_type=F32)


def _dot_nt(a, b):
    return lax.dot_general(a.astype(BF16), b.astype(BF16), (((1,), (1,)), ((), ())),
                           preferred_element_type=F32)


def _dot_tn(a, b):
    return lax.dot_general(a.astype(BF16), b.astype(BF16), (((0,), (0,)), ((), ())),
                           preferred_element_type=F32)


def _head_sum(x, m0):
    s0 = jnp.sum(jnp.where(m0, x, 0.0), axis=-1, keepdims=True)
    s1 = jnp.sum(jnp.where(m0, 0.0, x), axis=-1, keepdims=True)
    return jnp.where(m0, s0, s1)


def _split_heads(x, m0):
    x0 = jnp.where(m0, x, 0.0)
    return jnp.concatenate([x0, x - x0], axis=0)


def _rwkv_pair(r, k, v, wl, al, g, w0, a0, k_k, k_a, r_k, lnx_w, lnx_b, S):
    C = RW_CHUNK
    H = RW_HEAD
    lane = lax.broadcasted_iota(jnp.int32, (1, LANES), 1)
    m0 = lane < H

    z = -(w0 + wl)
    softplus = jnp.maximum(z, 0.0) + jnp.log(1.0 + jnp.exp(-jnp.abs(z)))
    log_decay = -jnp.exp(-softplus - 0.5)
    a = 1.0 / (1.0 + jnp.exp(-(a0 + al)))
    kk = k * k_k
    kk = kk / jnp.maximum(jnp.sqrt(_head_sum(kk * kk, m0)), 1e-12)
    k2 = k * (1.0 + (a - 1.0) * k_a)
    b = kk * a

    ti = lax.broadcasted_iota(jnp.int32, (C, C), 0)
    si = lax.broadcasted_iota(jnp.int32, (C, C), 1)
    tri = jnp.where(ti >= si, 1.0, 0.0).astype(BF16)
    p_hi = log_decay.astype(BF16)
    rem = log_decay - p_hi.astype(F32)
    p_mid = rem.astype(BF16)
    p_lo = (rem - p_mid.astype(F32)).astype(BF16)
    cum3 = jnp.dot(tri, jnp.concatenate([p_hi, p_mid, p_lo], axis=1), preferred_element_type=F32)
    cum = cum3[:, :LANES] + cum3[:, LANES:2 * LANES] + cum3[:, 2 * LANES:]

    mid = cum[C // 2 - 1:C // 2, :]
    last = cum[C - 1:C, :]
    e_fwd = jnp.exp(cum - mid)
    e_inv = jnp.exp(mid - cum)
    e_prev = jnp.exp(cum - log_decay - mid)
    d_mid = jnp.exp(mid)
    d_end_mid = jnp.exp(last - mid)
    d_end = jnp.exp(last)

    kk_t = kk * e_prev
    r_t = r * e_fwd
    b_t = b * e_inv
    k_t = k2 * e_inv
    q = jnp.concatenate([kk_t, r_t], axis=0)
    q0 = jnp.where(m0, q, 0.0)
    g0 = _dot_nt(q0, jnp.concatenate([b_t, k_t], axis=0))
    g1 = _dot_nt(q - q0, jnp.concatenate([k_t, b_t], axis=0))

    t_row = lax.broadcasted_iota(jnp.int32, (C, LANES), 0)
    s_col = lax.broadcasted_iota(jnp.int32, (C, LANES), 1) & (C - 1)
    strict = t_row > s_col
    incl = t_row >= s_col
    g0t, g0b, g1t, g1b = g0[:C], g0[C:], g1[:C], g1[C:]
    l_cat = jnp.where(strict, jnp.where(m0, g0t, g1t), 0.0)
    ak_cat = jnp.where(strict, jnp.where(m0, g1t, g0t), 0.0)
    gb0 = jnp.where(incl, g0b, 0.0)
    gb1 = jnp.where(incl, g1b, 0.0)
    grb_cat = jnp.where(m0, gb0, gb1)

    rr = lax.broadcasted_iota(jnp.int32, (LANES, LANES), 0)
    cc = lax.broadcasted_iota(jnp.int32, (LANES, LANES), 1)
    bd_mask = (rr < H) == (cc < H)

    def block_diag(x_cat):
        return jnp.where(bd_mask, jnp.concatenate([x_cat, x_cat], axis=0), 0.0)

    t_cat = jnp.where(t_row == s_col, 1.0, 0.0) - l_cat
    m_cat = l_cat
    n = 1
    while 2 * n < C:
        m_cat = _dot(m_cat, block_diag(m_cat))
        t_cat = t_cat + _dot(t_cat, block_diag(m_cat))
        n *= 2

    v0 = jnp.where(m0, v, 0.0)
    v1 = v - v0
    kk_h = _dot(t_cat, _split_heads(kk_t, m0))
    av = _dot(ak_cat, jnp.concatenate([v1, v0], axis=0))
    u0 = -_dot(t_cat, _split_heads(av, m0))
    r_h = r_t - _dot(grb_cat, _split_heads(kk_h, m0))
    u00 = jnp.where(m0, u0, 0.0)
    y0 = _dot(jnp.concatenate([gb0, gb1], axis=1),
              jnp.concatenate([u00, v0, v1, u0 - u00], axis=0))
    b_end = b_t * d_end_mid
    k_end = k_t * d_end_mid
    s_add = jnp.where(bd_mask, _dot_tn(jnp.concatenate([u0, v], axis=0),
                                       jnp.concatenate([b_end, k_end], axis=0)), 0.0)
    kb = jnp.where(bd_mask, _dot_tn(kk_h, b_end), 0.0)

    s_mid = S * d_mid
    y = _dot_nt(r_h, s_mid) + y0
    s_new = S * d_end - _dot(s_mid, kb) + s_add

    inv_h = 1.0 / H
    mu = _head_sum(y, m0) * inv_h
    yc = y - mu
    var = _head_sum(yc * yc, m0) * inv_h
    yn = yc * lax.rsqrt(var + GN_EPS) * lnx_w + lnx_b
    yn = yn + _head_sum(r * k2 * r_k, m0) * v
    return yn * g, s_new


def _rwkv_chunk_body(r_ref, k_ref, v_ref, wl_ref, al_ref, g_ref,
                     w0_ref, a0_ref, kk_ref, ka_ref, rk_ref, lw_ref, lb_ref,
                     y_ref, s_out_ref, s_ref, *, n_pairs, n_chunks):
    c = pl.program_id(2)

    @pl.when(c == 0)
    def _():
        s_ref[...] = jnp.zeros_like(s_ref)

    for p in range(n_pairs):
        sl = slice(p * LANES, (p + 1) * LANES)
        yg, s_new = _rwkv_pair(
            r_ref[0, :, sl], k_ref[0, :, sl], v_ref[0, :, sl], wl_ref[0, :, sl],
            al_ref[0, :, sl], g_ref[0, :, sl],
            w0_ref[:, sl], a0_ref[:, sl], kk_ref[:, sl], ka_ref[:, sl], rk_ref[:, sl],
            lw_ref[:, sl], lb_ref[:, sl], s_ref[p])
        y_ref[0, :, sl] = yg.astype(y_ref.dtype)
        s_ref[p] = s_new

    @pl.when(c == n_chunks - 1)
    def _():
        s_out_ref[0] = s_ref[...]


def _rwkv_prompt(r, k, v, wl, al, g, w0, a0, k_k, k_a, r_k, lnx_w, lnx_b):
    B, T, D = r.shape
    C, LW = RW_CHUNK, RW_LANES_PER_STEP
    n_pairs = LW // LANES
    n_chunks = T // C
    tok = pl.BlockSpec((1, C, LW), lambda b, h, c: (b, c, h))
    par = pl.BlockSpec((1, LW), lambda b, h, c: (0, h))
    body = functools.partial(_rwkv_chunk_body, n_pairs=n_pairs, n_chunks=n_chunks)
    y, s_bd = pl.pallas_call(
        body,
        grid=(B, D // LW, n_chunks),
        in_specs=[tok] * 6 + [par] * 7,
        out_specs=[pl.BlockSpec((1, C, LW), lambda b, h, c: (b, c, h)),
                   pl.BlockSpec((1, n_pairs, LANES, LANES), lambda b, h, c: (b, h, 0, 0))],
        out_shape=[jax.ShapeDtypeStruct((B, T, D), BF16),
                   jax.ShapeDtypeStruct((B, D // LANES, LANES, LANES), F32)],
        scratch_shapes=[pltpu.VMEM((n_pairs, LANES, LANES), F32)],
        compiler_params=_cparams(("parallel", "parallel", "arbitrary")),
        name="rwkv_chunk",
    )(r, k, v, wl, al, g, *[p.reshape(1, D) for p in (w0, a0, k_k, k_a, r_k, lnx_w, lnx_b)])
    H = RW_HEAD
    s = jnp.stack([s_bd[:, :, :H, :H], s_bd[:, :, H:, H:]], axis=2)
    return y, s.reshape(B, D // H, H, H)


def _rwkv_step_body(s_ref, d_ref, kk_ref, b_ref, k_ref, r_ref, v_ref, s_out_ref, y_ref):
    S = s_ref[0]
    s_kk = jnp.sum(S * kk_ref[0], axis=-1, keepdims=True)
    S = S * d_ref[0] - s_kk * b_ref[0] + v_ref[0] * k_ref[0]
    s_out_ref[0] = S
    y_ref[0] = jnp.sum(S * r_ref[0], axis=-1, keepdims=True)


def _rwkv_step(S0, decay, kk, b, k, r, v):
    B, Hh, N, _ = S0.shape
    row = lambda t: t.reshape(B, Hh, 1, N)
    col = lambda t: t.reshape(B, Hh, N, 1)
    row_spec = pl.BlockSpec((1, Hh, 1, N), lambda i: (i, 0, 0, 0))
    col_spec = pl.BlockSpec((1, Hh, N, 1), lambda i: (i, 0, 0, 0))
    st_spec = pl.BlockSpec((1, Hh, N, N), lambda i: (i, 0, 0, 0))
    S1, y = pl.pallas_call(
        _rwkv_step_body,
        grid=(B,),
        in_specs=[st_spec] + [row_spec] * 5 + [col_spec],
        out_specs=[st_spec, col_spec],
        out_shape=[jax.ShapeDtypeStruct(S0.shape, F32), jax.ShapeDtypeStruct((B, Hh, N, 1), F32)],
        compiler_params=_cparams(("parallel",)),
        name="rwkv_step",
    )(S0, row(decay), row(kk), row(b), row(k), row(r), col(v))
    return S1, y.reshape(B, Hh * N)


def _swa_prompt_body(q_ref, kp_ref, kc_ref, vp_ref, vc_ref, qg_ref, sink_ref, o_ref):
    i = pl.program_id(1)
    W = WINDOW
    lane = lax.broadcasted_iota(jnp.int32, (1, LANES), 1)
    m0 = lane < HEAD_DIM
    a_idx = lax.broadcasted_iota(jnp.int32, (W, 2 * W), 0)
    c_idx = lax.broadcasted_iota(jnp.int32, (W, 2 * W), 1)
    first_key = jnp.where(i > 0, 0, W)
    mask = (c_idx > a_idx) & (c_idx <= a_idx + W) & (c_idx >= first_key)
    neg = jnp.float32(-jnp.inf)
    qg = qg_ref[...]

    for blk in range(N_KV_HEADS // 2):
        ksl = slice(blk * LANES, (blk + 1) * LANES)
        kb = jnp.concatenate([kp_ref[0, :, ksl], kc_ref[0, :, ksl]], axis=0)
        vb = jnp.concatenate([vp_ref[0, :, ksl], vc_ref[0, :, ksl]], axis=0)
        kb_r = pltpu.roll(kb, HEAD_DIM, 1)
        vb_r = pltpu.roll(vb, HEAD_DIM, 1)
        for sub in range(2):
            if sub == 0:
                k2 = jnp.where(m0, kb, kb_r)
                v2 = jnp.where(m0, vb, vb_r)
            else:
                k2 = jnp.where(m0, kb_r, kb)
                v2 = jnp.where(m0, vb_r, vb)
            k2 = k2.astype(BF16)
            v2 = v2.astype(BF16)
            kv = 2 * blk + sub
            for pp in range(GQA_GROUP // 2):
                pair = kv * (GQA_GROUP // 2) + pp
                qsl = slice(pair * LANES, (pair + 1) * LANES)
                qv = q_ref[0, :, qsl]
                ms = _head_sum(qv * qv, m0) * (1.0 / HEAD_DIM)
                qn = qv * lax.rsqrt(ms + RMS_EPS) * qg * ATTN_SCALE
                sv = sink_ref[:, qsl]
                outs = []
                for h in range(2):
                    hm = m0 if h == 0 else jnp.logical_not(m0)
                    s = lax.dot_general(jnp.where(hm, qn, 0.0).astype(BF16), k2,
                                        (((1,), (1,)), ((), ())), preferred_element_type=F32)
                    s = jnp.where(mask, s, neg)
                    sink = jnp.max(jnp.where(hm, sv, neg), axis=-1, keepdims=True)
                    m = jnp.maximum(jnp.max(s, axis=-1, keepdims=True), sink)
                    p = jnp.exp(s - m)
                    denom = jnp.sum(p, axis=-1, keepdims=True) + jnp.exp(sink - m)
                    p = p / denom
                    outs.append(jnp.dot(p.astype(BF16), v2, preferred_element_type=F32))
                o_ref[0, :, qsl] = jnp.where(m0, outs[0], outs[1]).astype(o_ref.dtype)


def _swa_prompt(q, k, v, q_norm_g, sinks):
    B, T, D = q.shape
    KVW = k.shape[-1]
    W = WINDOW
    qg = jnp.tile(q_norm_g.astype(F32), 2).reshape(1, LANES)
    sink_l = jnp.repeat(sinks.astype(F32), HEAD_DIM).reshape(1, D)
    cur = pl.BlockSpec((1, W, KVW), lambda b, i: (b, i, 0))
    prev = pl.BlockSpec((1, W, KVW), lambda b, i: (b, jnp.maximum(i - 1, 0), 0))
    return pl.pallas_call(
        _swa_prompt_body,
        grid=(B, T // W),
        in_specs=[pl.BlockSpec((1, W, D), lambda b, i: (b, i, 0)), prev, cur, prev, cur,
                  pl.BlockSpec((1, LANES), lambda b, i: (0, 0)),
                  pl.BlockSpec((1, D), lambda b, i: (0, 0))],
        out_specs=pl.BlockSpec((1, W, D), lambda b, i: (b, i, 0)),
        out_shape=jax.ShapeDtypeStruct((B, T, D), BF16),
        compiler_params=_cparams(("parallel", "parallel")),
        name="swa_prompt",
    )(q, k, k, v, v, qg, sink_l)


def _swa_sample_body(q_ref, kc_ref, vc_ref, kn_ref, vn_ref, qg_ref, sink_ref, o_ref):
    G, HD = GQA_GROUP, HEAD_DIM
    q = q_ref[0]
    ms = jnp.mean(q * q, axis=-1, keepdims=True)
    qn = q * lax.rsqrt(ms + RMS_EPS) * qg_ref[...] * ATTN_SCALE
    n_buf = kc_ref.shape[1]
    col = lax.broadcasted_iota(jnp.int32, (G, n_buf), 1)
    valid = (n_buf - col) < WINDOW
    neg = jnp.float32(-jnp.inf)
    for kv in range(N_KV_HEADS):
        qj = qn[kv * G:(kv + 1) * G, :]
        lsl = slice(kv * HD, (kv + 1) * HD)
        kc = kc_ref[0, :, lsl]
        vc = vc_ref[0, :, lsl]
        kn = kn_ref[0, :, lsl]
        vn = vn_ref[0, :, lsl]
        s_c = jnp.where(valid, _dot_nt(qj, kc), neg)
        s_n = jnp.sum(qj.astype(BF16).astype(F32) * kn.astype(BF16).astype(F32),
                      axis=-1, keepdims=True)
        sink = sink_ref[kv * G:(kv + 1) * G, :]
        m = jnp.maximum(jnp.maximum(jnp.max(s_c, axis=-1, keepdims=True), s_n), sink)
        p_c = jnp.exp(s_c - m)
        p_n = jnp.exp(s_n - m)
        denom = jnp.sum(p_c, axis=-1, keepdims=True) + p_n + jnp.exp(sink - m)
        p_c = p_c / denom
        p_n = p_n / denom
        o = _dot(p_c, vc) + p_n.astype(BF16).astype(F32) * vn.astype(BF16).astype(F32)
        o_ref[0, kv * G:(kv + 1) * G, :] = o.astype(o_ref.dtype)


def _swa_sample(q, k_buf, v_buf, k_new, v_new, q_norm_g, sinks):
    B, D = q.shape
    n_buf, KVW = k_buf.shape[1:]
    NH, HD = N_Q_HEADS, HEAD_DIM
    buf = pl.BlockSpec((1, n_buf, KVW), lambda b: (b, 0, 0))
    new = pl.BlockSpec((1, 1, KVW), lambda b: (b, 0, 0))
    o = pl.pallas_call(
        _swa_sample_body,
        grid=(B,),
        in_specs=[pl.BlockSpec((1, NH, HD), lambda b: (b, 0, 0)), buf, buf, new, new,
                  pl.BlockSpec((1, HD), lambda b: (0, 0)),
                  pl.BlockSpec((NH, 1), lambda b: (0, 0))],
        out_specs=pl.BlockSpec((1, NH, HD), lambda b: (b, 0, 0)),
        out_shape=jax.ShapeDtypeStruct((B, NH, HD), BF16),
        compiler_params=_cparams(("parallel",)),
        name="swa_sample",
    )(q.reshape(B, NH, HD), k_buf, v_buf, k_new, v_new,
      q_norm_g.astype(F32).reshape(1, HD), sinks.astype(F32).reshape(NH, 1))
    return o.reshape(B, D)


def _rms(x, g):
    return x * lax.rsqrt(jnp.mean(x * x, axis=-1, keepdims=True) + RMS_EPS) * g


def _silu(x):
    return x * jax.nn.sigmoid(x)


def _forward(x, mods, kvmod, wkv0, shift0, conv0, k_buf, v_buf, p, wb, tag):
    B, T, D = x.shape
    M = B * T
    depth = len(mods)
    n_a = depth // 2
    new_wkv, new_shift, new_conv = [], [], []
    k_att = v_att = k_win = v_win = None
    x2 = x.reshape(M, D)
    for l in range(depth):
        sh1, sc1, gt1, sh2, sc2, gt2 = jnp.split(mods[l], 6, axis=-1)
        h = _rms(x2.reshape(B, T, D), p['ln1_g'][l]) * (1 + sc1[:, None]) + sh1[:, None]
        if l < n_a:
            h_prev = jnp.concatenate([shift0[l][:, None, :], h[:, :-1]], axis=1)
            dx = h_prev - h
            mix = p['rwkv_mix'][l]
            xr, xw, xk, xv, xa, xg = [(h + dx * mix[i]).astype(BF16).reshape(M, D) for i in range(6)]
            r = _mm(xr, wb['rwkv_w_r'][l], name=f"{tag}_wr")
            k = _mm(xk, wb['rwkv_w_k'][l], name=f"{tag}_wk")
            v = _mm(xv, wb['rwkv_w_v'][l], name=f"{tag}_wv")
            wl = _mm(_mm(xw, wb['rwkv_w1'][l], act="tanh", out_dtype=BF16, name=f"{tag}_w1"),
                     wb['rwkv_w2'][l], name=f"{tag}_w2")
            al = _mm(_mm(xa, wb['rwkv_a1'][l], out_dtype=BF16, name=f"{tag}_a1"),
                     wb['rwkv_a2'][l], name=f"{tag}_a2")
            g = _mm(_mm(xg, wb['rwkv_g1'][l], act="sigmoid", out_dtype=BF16, name=f"{tag}_g1"),
                    wb['rwkv_g2'][l], name=f"{tag}_g2")
            w0, a0 = p['rwkv_w0'][l], p['rwkv_a0'][l]
            k_k, k_a = p['rwkv_k_k'][l], p['rwkv_k_a'][l]
            r_k = p['rwkv_r_k'][l].reshape(D)
            lnx_w, lnx_b = p['rwkv_lnx_w'][l], p['rwkv_lnx_b'][l]
            if T > 1:
                sh = (B, T, D)
                yg, S = _rwkv_prompt(r.reshape(sh), k.reshape(sh), v.reshape(sh), wl.reshape(sh),
                                     al.reshape(sh), g.reshape(sh), w0, a0, k_k, k_a, r_k,
                                     lnx_w, lnx_b)
                yg = yg.reshape(M, D)
            else:
                Hh, N = RW_HEADS, RW_HEAD
                w_log = -jax.nn.softplus(-(w0 + wl)) - 0.5
                decay = jnp.exp(-jnp.exp(w_log))
                a = jax.nn.sigmoid(a0 + al)
                kk = (k * k_k).reshape(B, Hh, N)
                kk = kk / jnp.maximum(jnp.sqrt(jnp.sum(kk * kk, axis=-1, keepdims=True)), 1e-12)
                kk = kk.reshape(B, D)
                k2 = k * (1.0 + (a - 1.0) * k_a)
                S, y = _rwkv_step(wkv0[l].astype(F32), decay, kk, kk * a, k2, r, v)
                y4 = y.reshape(B, Hh, N)
                mu = jnp.mean(y4, axis=-1, keepdims=True)
                var = jnp.mean(jnp.square(y4 - mu), axis=-1, keepdims=True)
                y4 = ((y4 - mu) * lax.rsqrt(var + GN_EPS) * lnx_w.reshape(Hh, N)
                      + lnx_b.reshape(Hh, N))
                bonus = jnp.sum((r * k2 * r_k).reshape(B, Hh, N), axis=-1, keepdims=True)
                y4 = y4 + bonus * v.reshape(B, Hh, N)
                yg = (y4.reshape(B, D) * g).astype(BF16)
            new_wkv.append(S)
            new_shift.append(h[:, -1])
            x2 = _mm(yg, wb['rwkv_w_o'][l], res=x2, gate=gt1, rows_per_gate=T, name=f"{tag}_wo")
        else:
            j = l - n_a
            q = _mm(h.astype(BF16).reshape(M, D), wb['attn_w_q'][j], name=f"{tag}_wq")
            if k_buf is None:
                attn = _swa_prompt(q.reshape(B, T, D), k_att, v_att, p['attn_q_norm_g'][j],
                                   p['attn_sinks'][j]).reshape(M, D)
            else:
                attn = _swa_sample(q, k_buf, v_buf, k_att, v_att, p['attn_q_norm_g'][j],
                                   p['attn_sinks'][j])
            x2 = _mm(attn, wb['attn_w_o'][j], res=x2, gate=gt1, rows_per_gate=T,
                     name=f"{tag}_awo")
        h2 = _rms(x2.reshape(B, T, D), p['ln2_g'][l]) * (1 + sc2[:, None]) + sh2[:, None]
        gu = _mm(h2.astype(BF16).reshape(M, D), wb['ffn_w_in'][l], name=f"{tag}_win")
        F = gu.shape[-1] // 2
        gate = gu[:, :F].reshape(B, T, F)
        up = gu[:, F:].reshape(B, T, F)
        pad = jnp.concatenate([conv0[l].astype(F32), gate], axis=1)
        cw = p['ffn_conv_w'][l]
        conv = p['ffn_conv_b'][l] + sum(pad[:, jj:jj + T] * cw[jj] for jj in range(CONV_W))
        act = (_silu(conv) * up).astype(BF16).reshape(M, F)
        new_conv.append(pad[:, -(CONV_W - 1):])
        x2 = _mm(act, wb['ffn_w_out'][l], res=x2, gate=gt2, rows_per_gate=T, name=f"{tag}_wout")
        if l == n_a - 1:
            sh, sc = jnp.split(kvmod, 2, axis=-1)
            hn = _rms(x2.reshape(B, T, D), p['kv_norm_g']) * (1 + sc[:, None]) + sh[:, None]
            kv = _mm(hn.astype(BF16).reshape(M, D), wb['w_kv'], name=f"{tag}_wkv")
            KVW = kv.shape[-1] // 2
            k_new = _rms(kv[:, :KVW].reshape(B, T, N_KV_HEADS, HEAD_DIM), p['k_norm_g'])
            v_new = kv[:, KVW:].reshape(B, T, N_KV_HEADS, HEAD_DIM)
            if k_buf is None:
                w = min(WINDOW, T)
                k_att, v_att = k_new.reshape(B, T, KVW), v_new.reshape(B, T, KVW)
                k_win, v_win = k_new[:, T - w:], v_new[:, T - w:]
            else:
                n_buf = k_buf.shape[1]
                k_win = jnp.concatenate([k_buf, k_new], axis=1)[:, -n_buf:]
                v_win = jnp.concatenate([v_buf, v_new], axis=1)[:, -n_buf:]
                k_att, v_att = k_new.reshape(B, T, KVW), v_new.reshape(B, T, KVW)
                k_buf = k_buf.reshape(B, n_buf, KVW)
                v_buf = v_buf.reshape(B, n_buf, KVW)
    return (x2.reshape(B, T, D), jnp.stack(new_wkv), jnp.stack(new_shift), jnp.stack(new_conv),
            k_win, v_win)


_BIG_WEIGHTS = ('rwkv_w1', 'rwkv_w2', 'rwkv_a1', 'rwkv_a2', 'rwkv_g1', 'rwkv_g2',
                'rwkv_w_r', 'rwkv_w_k', 'rwkv_w_v', 'rwkv_w_o', 'w_kv',
                'attn_w_q', 'attn_w_o', 'ffn_w_in', 'ffn_w_out')


def kernel(x_prompt, x_sample, c_prompt, c_sample, state_wkv, state_shift, state_conv, cache_k_win, cache_v_win, mod_w, mod_b, ln1_g, ln2_g, rwkv_mix, rwkv_w0, rwkv_w1, rwkv_w2, rwkv_a0, rwkv_a1, rwkv_a2, rwkv_g1, rwkv_g2, rwkv_k_k, rwkv_k_a, rwkv_r_k, rwkv_w_r, rwkv_w_k, rwkv_w_v, rwkv_w_o, rwkv_lnx_w, rwkv_lnx_b, kv_norm_g, kv_mod_w, kv_mod_b, w_kv, k_norm_g, attn_w_q, attn_q_norm_g, attn_sinks, attn_w_o, ffn_w_in, ffn_conv_w, ffn_conv_b, ffn_w_out):
    p = dict(mod_w=mod_w, mod_b=mod_b, ln1_g=ln1_g, ln2_g=ln2_g,
             rwkv_mix=rwkv_mix, rwkv_w0=rwkv_w0, rwkv_w1=rwkv_w1, rwkv_w2=rwkv_w2,
             rwkv_a0=rwkv_a0, rwkv_a1=rwkv_a1, rwkv_a2=rwkv_a2, rwkv_g1=rwkv_g1, rwkv_g2=rwkv_g2,
             rwkv_k_k=rwkv_k_k, rwkv_k_a=rwkv_k_a, rwkv_r_k=rwkv_r_k, rwkv_w_r=rwkv_w_r,
             rwkv_w_k=rwkv_w_k, rwkv_w_v=rwkv_w_v, rwkv_w_o=rwkv_w_o,
             rwkv_lnx_w=rwkv_lnx_w, rwkv_lnx_b=rwkv_lnx_b,
             kv_norm_g=kv_norm_g, kv_mod_w=kv_mod_w, kv_mod_b=kv_mod_b, w_kv=w_kv, k_norm_g=k_norm_g,
             attn_w_q=attn_w_q, attn_q_norm_g=attn_q_norm_g, attn_sinks=attn_sinks, attn_w_o=attn_w_o,
             ffn_w_in=ffn_w_in, ffn_conv_w=ffn_conv_w, ffn_conv_b=ffn_conv_b, ffn_w_out=ffn_w_out)
    wb = {name: p[name].astype(BF16) for name in _BIG_WEIGHTS}
    depth = mod_w.shape[0]
    n_a = depth // 2
    Bp = x_prompt.shape[0]
    Bs = x_sample.shape[0]
    dt = x_prompt.dtype

    c_all = _silu(jnp.concatenate([c_prompt, c_sample], axis=0)).astype(BF16)
    mods = [_mm(c_all, mod_w[l], bias=mod_b[l], name=f"mod{l}") for l in range(depth)]
    kvmod = _mm(c_all, kv_mod_w, bias=kv_mod_b, name="kvmod")

    F = ffn_conv_b.shape[-1]
    wkv0 = jnp.zeros((n_a, Bp, RW_HEADS, RW_HEAD, RW_HEAD), dt)
    shift0 = jnp.zeros((n_a, Bp, D_MODEL), dt)
    conv0 = jnp.zeros((depth, Bp, CONV_W - 1, F), dt)
    y_p, wkv_p, shift_p, conv_p, kwin_p, vwin_p = _forward(
        x_prompt, [m[:Bp] for m in mods], kvmod[:Bp], wkv0, shift0, conv0, None, None, p, wb, "p")
    y_s, wkv_s, shift_s, conv_s, kwin_s, vwin_s = _forward(
        x_sample, [m[Bp:] for m in mods], kvmod[Bp:], state_wkv, state_shift, state_conv,
        cache_k_win, cache_v_win, p, wb, "s")
    return (y_p, y_s, wkv_p, wkv_s, shift_p, shift_s, conv_p, conv_s,
            kwin_p, kwin_s, vwin_p, vwin_s)
```

```python
import functools

import jax
import jax.numpy as jnp
from jax import lax
from jax.experimental import pallas as pl
from jax.experimental.pallas import tpu as pltpu

F32 = jnp.float32
BF16 = jnp.bfloat16

D_MODEL = 4096
RW_HEAD = 64
RW_HEADS = D_MODEL // RW_HEAD
GN_EPS = 64e-5
RMS_EPS = 1e-6
HEAD_DIM = 64
N_Q_HEADS = D_MODEL // HEAD_DIM
N_KV_HEADS = N_Q_HEADS // 8
GQA_GROUP = N_Q_HEADS // N_KV_HEADS
WINDOW = 128
ATTN_SCALE = HEAD_DIM ** -0.5
CONV_W = 3

LANES = 128
SUBLANES = 8
VMEM_LIMIT_BYTES = 56 * 1024 * 1024

RW_CHUNK = 64
RW_LANES_PER_STEP = 1024
NORM_ROWS = 256


def _cparams(sem):
    return pltpu.CompilerParams(dimension_semantics=sem, vmem_limit_bytes=VMEM_LIMIT_BYTES)


def _pick_tile(dim, prefs):
    for p in prefs:
        if dim >= p and dim % p == 0:
            return p
    return dim


def _sigmoid(x):
    return 1.0 / (1.0 + jnp.exp(-x))


def _mm_body(*refs, nk, has_bias, act, has_res, emit_w):
    x_ref, w_ref = refs[0], refs[1]
    pos = 2
    bias_ref = res_ref = gate_ref = wout_ref = None
    if has_bias:
        bias_ref = refs[pos]; pos += 1
    if has_res:
        res_ref, gate_ref = refs[pos], refs[pos + 1]; pos += 2
    o_ref = refs[pos]; pos += 1
    if emit_w:
        wout_ref = refs[pos]; pos += 1
    acc_ref = refs[pos] if nk > 1 else None

    w = w_ref[...].astype(BF16)
    if emit_w:
        wout_ref[...] = w
    part = jnp.dot(x_ref[...].astype(BF16), w, preferred_element_type=F32)

    def finish(acc):
        if has_bias:
            acc = acc + bias_ref[...]
        if act == "tanh":
            acc = jnp.tanh(acc)
        elif act == "sigmoid":
            acc = _sigmoid(acc)
        if has_res:
            acc = res_ref[...] + gate_ref[...] * acc
        o_ref[...] = acc.astype(o_ref.dtype)

    if nk == 1:
        finish(part)
    else:
        k = pl.program_id(2)

        @pl.when(k == 0)
        def _():
            acc_ref[...] = part

        @pl.when(k > 0)
        def _():
            acc_ref[...] += part

        @pl.when(k == nk - 1)
        def _():
            finish(acc_ref[...])


def _mm(x, w, *, bias=None, act=None, res=None, gate=None, rows_per_gate=1,
        out_dtype=F32, emit_w=False, name="mm"):
    M, K = x.shape
    K2, N = w.shape
    assert K == K2
    tm = _pick_tile(M, (1024,))
    tn = _pick_tile(N, (1024, 512))
    tk = _pick_tile(K, (2048,))
    nk = K // tk
    grid = (M // tm, N // tn, nk)
    assert not emit_w or grid[0] == 1

    in_specs = [pl.BlockSpec((tm, tk), lambda i, j, k: (i, k)),
                pl.BlockSpec((tk, tn), lambda i, j, k: (k, j))]
    args = [x, w]
    if bias is not None:
        in_specs.append(pl.BlockSpec((1, tn), lambda i, j, k: (0, j)))
        args.append(bias.reshape(1, N).astype(F32))
    if res is not None:
        in_specs.append(pl.BlockSpec((tm, tn), lambda i, j, k: (i, j)))
        args.append(res)
        if rows_per_gate == 1:
            in_specs.append(pl.BlockSpec((tm, tn), lambda i, j, k: (i, j)))
            args.append(gate)
        else:
            assert rows_per_gate % tm == 0
            bpg = rows_per_gate // tm
            in_specs.append(pl.BlockSpec((None, 1, tn), lambda i, j, k: (i // bpg, 0, j)))
            args.append(gate.reshape(gate.shape[0], 1, N))
    out_specs = [pl.BlockSpec((tm, tn), lambda i, j, k: (i, j))]
    out_shape = [jax.ShapeDtypeStruct((M, N), out_dtype)]
    if emit_w:
        out_specs.append(pl.BlockSpec((tk, tn), lambda i, j, k: (k, j)))
        out_shape.append(jax.ShapeDtypeStruct((K, N), BF16))
    scratch = [pltpu.VMEM((tm, tn), F32)] if nk > 1 else []
    body = functools.partial(_mm_body, nk=nk, has_bias=bias is not None, act=act,
                             has_res=res is not None, emit_w=emit_w)
    outs = pl.pallas_call(
        body,
        grid=grid,
        in_specs=in_specs,
        out_specs=out_specs,
        out_shape=out_shape,
        scratch_shapes=scratch,
        compiler_params=_cparams(("parallel", "parallel", "arbitrary")),
        name=name,
    )(*args)
    return tuple(outs) if emit_w else outs[0]


def _accumulate(k, nk, pairs):
    if nk == 1:
        return

    @pl.when(k == 0)
    def _():
        for acc_ref, part in pairs:
            acc_ref[...] = part

    @pl.when(k > 0)
    def _():
        for acc_ref, part in pairs:
            acc_ref[...] += part


def _ffn_in_seq_body(x_ref, xh_ref, wg_ref, wu_ref, cw_ref, cb_ref, st_ref,
                     act_ref, st_out_ref, accg, accu, acch, *, nk, tiles_per_seq):
    i = pl.program_id(0)
    k = pl.program_id(2)
    x = x_ref[...]
    wg = wg_ref[...]
    pg = jnp.dot(x, wg, preferred_element_type=F32)
    pu = jnp.dot(x, wu_ref[...], preferred_element_type=F32)
    ph = jnp.dot(xh_ref[...], wg, preferred_element_type=F32)
    _accumulate(k, nk, [(accg, pg), (accu, pu), (acch, ph)])

    @pl.when(k == nk - 1)
    def _():
        g = accg[...] if nk > 1 else pg
        u = accu[...] if nk > 1 else pu
        halo = acch[...] if nk > 1 else ph
        tm, tn = g.shape
        seq_start = jnp.full((1, tn), i % tiles_per_seq, jnp.int32) == 0
        st = st_ref[...]
        before1 = jnp.where(seq_start, st[1:2], halo[SUBLANES - 1:SUBLANES])
        before2 = jnp.where(seq_start, st[0:1], halo[SUBLANES - 2:SUBLANES - 1])
        row = lax.broadcasted_iota(jnp.int32, (tm, tn), 0)
        prev1 = jnp.where(row == 0, before1, pltpu.roll(g, 1, 0))
        prev2 = jnp.where(row == 0, before2, jnp.where(row == 1, before1, pltpu.roll(g, 2, 0)))
        cw = cw_ref[...]
        conv = cb_ref[...] + cw[0:1] * prev2 + cw[1:2] * prev1 + cw[2:3] * g
        act_ref[...] = (conv * _sigmoid(conv) * u).astype(act_ref.dtype)
        st_out_ref[...] = g[tm - (CONV_W - 1):, :]


def _ffn_in_seq(x, wg, wu, conv_w, conv_b, state, T):
    M, K = x.shape
    F = wg.shape[1]
    B = M // T
    tm = _pick_tile(T, (1024,))
    tn = _pick_tile(F, (1024, 512))
    tk = _pick_tile(K, (2048,))
    nk = K // tk
    tps = T // tm
    hb = tm // SUBLANES
    body = functools.partial(_ffn_in_seq_body, nk=nk, tiles_per_seq=tps)
    w_spec = pl.BlockSpec((tk, tn), lambda i, j, k: (k, j))
    act, st = pl.pallas_call(
        body,
        grid=(M // tm, F // tn, nk),
        in_specs=[pl.BlockSpec((tm, tk), lambda i, j, k: (i, k)),
                  pl.BlockSpec((SUBLANES, tk), lambda i, j, k: (jnp.maximum(i * hb - 1, 0), k)),
                  w_spec, w_spec,
                  pl.BlockSpec((CONV_W, tn), lambda i, j, k: (0, j)),
                  pl.BlockSpec((1, tn), lambda i, j, k: (0, j)),
                  pl.BlockSpec((None, CONV_W - 1, tn), lambda i, j, k: (i // tps, 0, j))],
        out_specs=[pl.BlockSpec((tm, tn), lambda i, j, k: (i, j)),
                   pl.BlockSpec((None, CONV_W - 1, tn), lambda i, j, k: (i, 0, j))],
        out_shape=[jax.ShapeDtypeStruct((M, F), BF16),
                   jax.ShapeDtypeStruct((M // tm, CONV_W - 1, F), F32)],
        scratch_shapes=[pltpu.VMEM((tm, tn), F32), pltpu.VMEM((tm, tn), F32),
                        pltpu.VMEM((SUBLANES, tn), F32)],
        compiler_params=_cparams(("parallel", "parallel", "arbitrary")),
        name="ffn_in_seq",
    )(x, x, wg, wu, conv_w.astype(F32), conv_b.reshape(1, F).astype(F32), state.astype(F32))
    return act, st[tps - 1::tps]


def _ffn_in_tok_body(x_ref, wg_ref, wu_ref, cw_ref, cb_ref, s0_ref, s1_ref,
                     act_ref, g_ref, wgo_ref, wuo_ref, accg, accu, *, nk):
    k = pl.program_id(2)
    x = x_ref[...]
    wg = wg_ref[...].astype(BF16)
    wu = wu_ref[...].astype(BF16)
    wgo_ref[...] = wg
    wuo_ref[...] = wu
    pg = jnp.dot(x, wg, preferred_element_type=F32)
    pu = jnp.dot(x, wu, preferred_element_type=F32)
    _accumulate(k, nk, [(accg, pg), (accu, pu)])

    @pl.when(k == nk - 1)
    def _():
        g = accg[...] if nk > 1 else pg
        u = accu[...] if nk > 1 else pu
        cw = cw_ref[...]
        conv = cb_ref[...] + cw[0:1] * s0_ref[...] + cw[1:2] * s1_ref[...] + cw[2:3] * g
        act_ref[...] = (conv * _sigmoid(conv) * u).astype(act_ref.dtype)
        g_ref[...] = g


def _ffn_in_tok(x, w_in, conv_w, conv_b, state):
    B, K = x.shape
    F = w_in.shape[1] // 2
    tn = _pick_tile(F, (1024, 512))
    tk = _pick_tile(K, (2048,))
    nk = K // tk
    nj = F // tn
    body = functools.partial(_ffn_in_tok_body, nk=nk)
    row_spec = pl.BlockSpec((B, tn), lambda i, j, k: (0, j))
    wo_spec = pl.BlockSpec((tk, tn), lambda i, j, k: (k, j))
    s0 = state[:, 0, :].astype(F32)
    s1 = state[:, 1, :].astype(F32)
    act, g, wg, wu = pl.pallas_call(
        body,
        grid=(1, nj, nk),
        in_specs=[pl.BlockSpec((B, tk), lambda i, j, k: (0, k)),
                  pl.BlockSpec((tk, tn), lambda i, j, k: (k, j)),
                  pl.BlockSpec((tk, tn), lambda i, j, k: (k, j + nj)),
                  pl.BlockSpec((CONV_W, tn), lambda i, j, k: (0, j)),
                  pl.BlockSpec((1, tn), lambda i, j, k: (0, j)),
                  row_spec, row_spec],
        out_specs=[row_spec, row_spec, wo_spec, wo_spec],
        out_shape=[jax.ShapeDtypeStruct((B, F), BF16), jax.ShapeDtypeStruct((B, F), F32),
                   jax.ShapeDtypeStruct((K, F), BF16), jax.ShapeDtypeStruct((K, F), BF16)],
        scratch_shapes=[pltpu.VMEM((B, tn), F32), pltpu.VMEM((B, tn), F32)],
        compiler_params=_cparams(("parallel", "parallel", "arbitrary")),
        name="ffn_in_tok",
    )(x, w_in, w_in, conv_w.astype(F32), conv_b.reshape(1, F).astype(F32), s0, s1)
    return act, jnp.stack([s1, g], axis=1), wg, wu


def _norm_mod_rows(x, g, sc, sh):
    y = x * lax.rsqrt(jnp.mean(x * x, axis=-1, keepdims=True) + RMS_EPS) * g
    return y * (1.0 + sc) + sh


def _norm_body(x_ref, g_ref, sc_ref, sh_ref, o_ref):
    o_ref[0] = _norm_mod_rows(x_ref[0], g_ref[...], sc_ref[0], sh_ref[0]).astype(o_ref.dtype)


def _norm_mod(x, g, sc, sh):
    B, T, D = x.shape
    tr = _pick_tile(T, (NORM_ROWS,))
    tok = pl.BlockSpec((1, tr, D), lambda b, t: (b, t, 0))
    per_b = pl.BlockSpec((1, 1, D), lambda b, t: (b, 0, 0))
    return pl.pallas_call(
        _norm_body,
        grid=(B, T // tr),
        in_specs=[tok, pl.BlockSpec((1, D), lambda b, t: (0, 0)), per_b, per_b],
        out_specs=tok,
        out_shape=jax.ShapeDtypeStruct((B, T, D), BF16),
        compiler_params=_cparams(("parallel", "parallel")),
        name="norm_mod",
    )(x, g.reshape(1, D), sc.reshape(B, 1, D), sh.reshape(B, 1, D))


def _norm_shift_body(x_ref, g_ref, sc_ref, sh_ref, s0_ref, mix_ref, *rest, n_mix):
    outs = rest[:n_mix]
    last_ref = rest[n_mix]
    carry = rest[n_mix + 1]
    t = pl.program_id(1)

    @pl.when(t == 0)
    def _():
        carry[...] = s0_ref[0]

    h = _norm_mod_rows(x_ref[0], g_ref[...], sc_ref[0], sh_ref[0])
    tr = h.shape[0]
    if tr == 1:
        h_prev = carry[...]
    else:
        row = lax.broadcasted_iota(jnp.int32, h.shape, 0)
        h_prev = jnp.where(row == 0, carry[...], pltpu.roll(h, 1, 0))
    dx = h_prev - h
    mix = mix_ref[...]
    for i in range(n_mix):
        outs[i][0] = (h + dx * mix[i:i + 1]).astype(outs[i].dtype)
    last = h[tr - 1:tr, :]
    carry[...] = last
    last_ref[0] = last


def _norm_shift(x, g, sc, sh, shift_prev, mix):
    B, T, D = x.shape
    n_mix = mix.shape[0]
    tr = _pick_tile(T, (NORM_ROWS,))
    tok = pl.BlockSpec((1, tr, D), lambda b, t: (b, t, 0))
    per_b = pl.BlockSpec((1, 1, D), lambda b, t: (b, 0, 0))
    body = functools.partial(_norm_shift_body, n_mix=n_mix)
    outs = pl.pallas_call(
        body,
        grid=(B, T // tr),
        in_specs=[tok, pl.BlockSpec((1, D), lambda b, t: (0, 0)), per_b, per_b, per_b,
                  pl.BlockSpec((n_mix, D), lambda b, t: (0, 0))],
        out_specs=[tok] * n_mix + [per_b],
        out_shape=[jax.ShapeDtypeStruct((B, T, D), BF16)] * n_mix
        + [jax.ShapeDtypeStruct((B, 1, D), F32)],
        scratch_shapes=[pltpu.VMEM((1, D), F32)],
        compiler_params=_cparams(("parallel", "arbitrary")),
        name="norm_shift",
    )(x, g.reshape(1, D), sc.reshape(B, 1, D), sh.reshape(B, 1, D),
      shift_prev.astype(F32).reshape(B, 1, D), mix.astype(F32))
    return outs[:n_mix], outs[n_mix].reshape(B, D)


def _bdot(a, b):
    return lax.dot_general(a.astype(BF16), b.astype(BF16), (((2,), (1,)), ((0,), (0,))),
                           preferred_element_type=F32)


def _bdot_nt(a, b):
    return lax.dot_general(a.astype(BF16), b.astype(BF16), (((2,), (2,)), ((0,), (0,))),
                           preferred_element_type=F32)


def _bdot_tn(a, b):
    return lax.dot_general(a.astype(BF16), b.astype(BF16), (((1,), (1,)), ((0,), (0,))),
                           preferred_element_type=F32)


def _dot(a, b):
    return jnp.dot(a.astype(BF16), b.astype(BF16), preferred_element_type=F32)


def _dot_nt(a, b):
    return lax.dot_general(a.astype(BF16), b.astype(BF16), (((1,), (1,)), ((), ())),
                           preferred_element_type=F32)


def _head_sum(x, m0):
    s0 = jnp.sum(jnp.where(m0, x, 0.0), axis=-1, keepdims=True)
    s1 = jnp.sum(jnp.where(m0, 0.0, x), axis=-1, keepdims=True)
    return jnp.where(m0, s0, s1)


def _split_heads(x, m0):
    x0 = jnp.where(m0, x, 0.0)
    return jnp.concatenate([x0, x - x0], axis=-2)


def _rwkv_pairs(r, k, v, wl, al, g, w0, a0, k_k, k_a, r_k, lnx_w, lnx_b, S):
    C = RW_CHUNK
    H = RW_HEAD
    P = r.shape[0]
    lane = lax.broadcasted_iota(jnp.int32, (1, 1, LANES), 2)
    m0 = lane < H

    z = -(w0 + wl)
    softplus = jnp.maximum(z, 0.0) + jnp.log(1.0 + jnp.exp(-jnp.abs(z)))
    log_decay = -jnp.exp(-softplus - 0.5)
    a = 1.0 / (1.0 + jnp.exp(-(a0 + al)))
    kk = k * k_k
    kk = kk / jnp.maximum(jnp.sqrt(_head_sum(kk * kk, m0)), 1e-12)
    k2 = k * (1.0 + (a - 1.0) * k_a)
    b = kk * a

    ti = lax.broadcasted_iota(jnp.int32, (P, C, C), 1)
    si = lax.broadcasted_iota(jnp.int32, (P, C, C), 2)
    tri = jnp.where(ti >= si, 1.0, 0.0).astype(BF16)
    p_hi = log_decay.astype(BF16)
    rem = log_decay - p_hi.astype(F32)
    p_mid = rem.astype(BF16)
    p_lo = (rem - p_mid.astype(F32)).astype(BF16)
    cum3 = _bdot(tri, jnp.concatenate([p_hi, p_mid, p_lo], axis=2))
    cum = cum3[:, :, :LANES] + cum3[:, :, LANES:2 * LANES] + cum3[:, :, 2 * LANES:]

    mid = cum[:, C // 2 - 1:C // 2, :]
    last = cum[:, C - 1:C, :]
    e_fwd = jnp.exp(cum - mid)
    e_inv = jnp.exp(mid - cum)
    e_prev = jnp.exp(cum - log_decay - mid)
    d_mid = jnp.exp(mid)
    d_end_mid = jnp.exp(last - mid)
    d_end = jnp.exp(last)

    kk_t = kk * e_prev
    r_t = r * e_fwd
    b_t = b * e_inv
    k_t = k2 * e_inv
    q = jnp.concatenate([kk_t, r_t], axis=1)
    q0 = jnp.where(m0, q, 0.0)
    g0 = _bdot_nt(q0, jnp.concatenate([b_t, k_t], axis=1))
    g1 = _bdot_nt(q - q0, jnp.concatenate([k_t, b_t], axis=1))

    t_row = lax.broadcasted_iota(jnp.int32, (1, C, LANES), 1)
    s_col = lax.broadcasted_iota(jnp.int32, (1, C, LANES), 2) & (C - 1)
    strict = t_row > s_col
    incl = t_row >= s_col
    g0t, g0b, g1t, g1b = g0[:, :C], g0[:, C:], g1[:, :C], g1[:, C:]
    l_cat = jnp.where(strict, jnp.where(m0, g0t, g1t), 0.0)
    ak_cat = jnp.where(strict, jnp.where(m0, g1t, g0t), 0.0)
    gb0 = jnp.where(incl, g0b, 0.0)
    gb1 = jnp.where(incl, g1b, 0.0)
    grb_cat = jnp.where(m0, gb0, gb1)

    rr = lax.broadcasted_iota(jnp.int32, (1, LANES, LANES), 1)
    cc = lax.broadcasted_iota(jnp.int32, (1, LANES, LANES), 2)
    bd_mask = (rr < H) == (cc < H)

    def block_diag(x_cat):
        return jnp.where(bd_mask, jnp.concatenate([x_cat, x_cat], axis=1), 0.0)

    t_cat = jnp.where(t_row == s_col, 1.0, 0.0) - l_cat
    m_cat = l_cat
    n = 1
    while 2 * n < C:
        m_cat = _bdot(m_cat, block_diag(m_cat))
        t_cat = t_cat + _bdot(t_cat, block_diag(m_cat))
        n *= 2

    v0 = jnp.where(m0, v, 0.0)
    v1 = v - v0
    kk_h = _bdot(t_cat, _split_heads(kk_t, m0))
    av = _bdot(ak_cat, jnp.concatenate([v1, v0], axis=1))
    u0 = -_bdot(t_cat, _split_heads(av, m0))
    r_h = r_t - _bdot(grb_cat, _split_heads(kk_h, m0))
    u00 = jnp.where(m0, u0, 0.0)
    y0 = _bdot(jnp.concatenate([gb0, gb1], axis=2),
               jnp.concatenate([u00, v0, v1, u0 - u00], axis=1))
    b_end = b_t * d_end_mid
    k_end = k_t * d_end_mid
    s_add = jnp.where(bd_mask, _bdot_tn(jnp.concatenate([u0, v], axis=1),
                                        jnp.concatenate([b_end, k_end], axis=1)), 0.0)
    kb = jnp.where(bd_mask, _bdot_tn(kk_h, b_end), 0.0)

    s_mid = S * d_mid
    y = _bdot_nt(r_h, s_mid) + y0
    s_new = S * d_end - _bdot(s_mid, kb) + s_add

    inv_h = 1.0 / H
    mu = _head_sum(y, m0) * inv_h
    yc = y - mu
    var = _head_sum(yc * yc, m0) * inv_h
    yn = yc * lax.rsqrt(var + GN_EPS) * lnx_w + lnx_b
    yn = yn + _head_sum(r * k2 * r_k, m0) * v
    return yn * g, s_new


def _rwkv_chunk_body(r_ref, k_ref, v_ref, wl_ref, al_ref, g_ref,
                     w0_ref, a0_ref, kk_ref, ka_ref, rk_ref, lw_ref, lb_ref,
                     y_ref, s_out_ref, s_ref, *, n_pairs, n_chunks):
    c = pl.program_id(2)

    @pl.when(c == 0)
    def _():
        s_ref[...] = jnp.zeros_like(s_ref)

    def pairs(ref, lead):
        return jnp.stack([ref[lead + (slice(None), slice(p * LANES, (p + 1) * LANES))]
                          for p in range(n_pairs)], axis=0)

    tok = [pairs(ref, (0,)) for ref in (r_ref, k_ref, v_ref, wl_ref, al_ref, g_ref)]
    par = [pairs(ref, ()) for ref in (w0_ref, a0_ref, kk_ref, ka_ref, rk_ref, lw_ref, lb_ref)]
    yg, s_new = _rwkv_pairs(*tok, *par, s_ref[...])
    for p in range(n_pairs):
        y_ref[0, :, p * LANES:(p + 1) * LANES] = yg[p].astype(y_ref.dtype)
    s_ref[...] = s_new

    @pl.when(c == n_chunks - 1)
    def _():
        s_out_ref[0] = s_new


def _rwkv_prompt(r, k, v, wl, al, g, w0, a0, k_k, k_a, r_k, lnx_w, lnx_b):
    B, T, D = r.shape
    C, LW = RW_CHUNK, RW_LANES_PER_STEP
    n_pairs = LW // LANES
    n_chunks = T // C
    tok = pl.BlockSpec((1, C, LW), lambda b, h, c: (b, c, h))
    par = pl.BlockSpec((1, LW), lambda b, h, c: (0, h))
    body = functools.partial(_rwkv_chunk_body, n_pairs=n_pairs, n_chunks=n_chunks)
    y, s_bd = pl.pallas_call(
        body,
        grid=(B, D // LW, n_chunks),
        in_specs=[tok] * 6 + [par] * 7,
        out_specs=[pl.BlockSpec((1, C, LW), lambda b, h, c: (b, c, h)),
                   pl.BlockSpec((1, n_pairs, LANES, LANES), lambda b, h, c: (b, h, 0, 0))],
        out_shape=[jax.ShapeDtypeStruct((B, T, D), BF16),
                   jax.ShapeDtypeStruct((B, D // LANES, LANES, LANES), F32)],
        scratch_shapes=[pltpu.VMEM((n_pairs, LANES, LANES), F32)],
        compiler_params=_cparams(("parallel", "parallel", "arbitrary")),
        name="rwkv_chunk",
    )(r, k, v, wl, al, g, *[p.reshape(1, D) for p in (w0, a0, k_k, k_a, r_k, lnx_w, lnx_b)])
    H = RW_HEAD
    s = jnp.stack([s_bd[:, :, :H, :H], s_bd[:, :, H:, H:]], axis=2)
    return y, s.reshape(B, D // H, H, H)


def _rwkv_step_body(s_ref, d_ref, kk_ref, b_ref, k_ref, r_ref, v_ref, s_out_ref, y_ref):
    S = s_ref[0]
    s_kk = jnp.sum(S * kk_ref[0], axis=-1, keepdims=True)
    S = S * d_ref[0] - s_kk * b_ref[0] + v_ref[0] * k_ref[0]
    s_out_ref[0] = S
    y_ref[0] = jnp.sum(S * r_ref[0], axis=-1, keepdims=True)


def _rwkv_step(S0, decay, kk, b, k, r, v):
    B, Hh, N, _ = S0.shape
    row = lambda t: t.reshape(B, Hh, 1, N)
    col = lambda t: t.reshape(B, Hh, N, 1)
    row_spec = pl.BlockSpec((1, Hh, 1, N), lambda i: (i, 0, 0, 0))
    col_spec = pl.BlockSpec((1, Hh, N, 1), lambda i: (i, 0, 0, 0))
    st_spec = pl.BlockSpec((1, Hh, N, N), lambda i: (i, 0, 0, 0))
    S1, y = pl.pallas_call(
        _rwkv_step_body,
        grid=(B,),
        in_specs=[st_spec] + [row_spec] * 5 + [col_spec],
        out_specs=[st_spec, col_spec],
        out_shape=[jax.ShapeDtypeStruct(S0.shape, F32), jax.ShapeDtypeStruct((B, Hh, N, 1), F32)],
        compiler_params=_cparams(("parallel",)),
        name="rwkv_step",
    )(S0, row(decay), row(kk), row(b), row(k), row(r), col(v))
    return S1, y.reshape(B, Hh * N)


def _swa_prompt_body(q_ref, kp_ref, kc_ref, vp_ref, vc_ref, qg_ref, sink_ref, o_ref):
    i = pl.program_id(1)
    W = WINDOW
    lane = lax.broadcasted_iota(jnp.int32, (1, LANES), 1)
    m0 = lane < HEAD_DIM
    a_idx = lax.broadcasted_iota(jnp.int32, (W, 2 * W), 0)
    c_idx = lax.broadcasted_iota(jnp.int32, (W, 2 * W), 1)
    first_key = jnp.where(i > 0, 0, W)
    mask = (c_idx > a_idx) & (c_idx <= a_idx + W) & (c_idx >= first_key)
    neg = jnp.float32(-jnp.inf)
    qg = qg_ref[...]

    for blk in range(N_KV_HEADS // 2):
        ksl = slice(blk * LANES, (blk + 1) * LANES)
        kb = jnp.concatenate([kp_ref[0, :, ksl], kc_ref[0, :, ksl]], axis=0)
        vb = jnp.concatenate([vp_ref[0, :, ksl], vc_ref[0, :, ksl]], axis=0)
        kb_r = pltpu.roll(kb, HEAD_DIM, 1)
        vb_r = pltpu.roll(vb, HEAD_DIM, 1)
        for sub in range(2):
            if sub == 0:
                k2 = jnp.where(m0, kb, kb_r)
                v2 = jnp.where(m0, vb, vb_r)
            else:
                k2 = jnp.where(m0, kb_r, kb)
                v2 = jnp.where(m0, vb_r, vb)
            k2 = k2.astype(BF16)
            v2 = v2.astype(BF16)
            kv = 2 * blk + sub
            for pp in range(GQA_GROUP // 2):
                pair = kv * (GQA_GROUP // 2) + pp
                qsl = slice(pair * LANES, (pair + 1) * LANES)
                qv = q_ref[0, :, qsl]
                ms = _head_sum(qv * qv, m0) * (1.0 / HEAD_DIM)
                qn = qv * lax.rsqrt(ms + RMS_EPS) * qg * ATTN_SCALE
                sv = sink_ref[:, qsl]
                outs = []
                for h in range(2):
                    hm = m0 if h == 0 else jnp.logical_not(m0)
                    s = lax.dot_general(jnp.where(hm, qn, 0.0).astype(BF16), k2,
                                        (((1,), (1,)), ((), ())), preferred_element_type=F32)
                    s = jnp.where(mask, s, neg)
                    sink = jnp.max(jnp.where(hm, sv, neg), axis=-1, keepdims=True)
                    m = jnp.maximum(jnp.max(s, axis=-1, keepdims=True), sink)
                    p = jnp.exp(s - m)
                    denom = jnp.sum(p, axis=-1, keepdims=True) + jnp.exp(sink - m)
                    p = p / denom
                    outs.append(jnp.dot(p.astype(BF16), v2, preferred_element_type=F32))
                o_ref[0, :, qsl] = jnp.where(m0, outs[0], outs[1]).astype(o_ref.dtype)


def _swa_prompt(q, k, v, q_norm_g, sinks):
    B, T, D = q.shape
    KVW = k.shape[-1]
    W = WINDOW
    qg = jnp.tile(q_norm_g.astype(F32), 2).reshape(1, LANES)
    sink_l = jnp.repeat(sinks.astype(F32), HEAD_DIM).reshape(1, D)
    cur = pl.BlockSpec((1, W, KVW), lambda b, i: (b, i, 0))
    prev = pl.BlockSpec((1, W, KVW), lambda b, i: (b, jnp.maximum(i - 1, 0), 0))
    return pl.pallas_call(
        _swa_prompt_body,
        grid=(B, T // W),
        in_specs=[pl.BlockSpec((1, W, D), lambda b, i: (b, i, 0)), prev, cur, prev, cur,
                  pl.BlockSpec((1, LANES), lambda b, i: (0, 0)),
                  pl.BlockSpec((1, D), lambda b, i: (0, 0))],
        out_specs=pl.BlockSpec((1, W, D), lambda b, i: (b, i, 0)),
        out_shape=jax.ShapeDtypeStruct((B, T, D), BF16),
        compiler_params=_cparams(("parallel", "parallel")),
        name="swa_prompt",
    )(q, k, k, v, v, qg, sink_l)


def _swa_sample_body(q_ref, kc_ref, vc_ref, kn_ref, vn_ref, qg_ref, sink_ref, o_ref):
    G, HD = GQA_GROUP, HEAD_DIM
    q = q_ref[0]
    ms = jnp.mean(q * q, axis=-1, keepdims=True)
    qn = q * lax.rsqrt(ms + RMS_EPS) * qg_ref[...] * ATTN_SCALE
    n_buf = kc_ref.shape[1]
    col = lax.broadcasted_iota(jnp.int32, (G, n_buf), 1)
    valid = (n_buf - col) < WINDOW
    neg = jnp.float32(-jnp.inf)
    for kv in range(N_KV_HEADS):
        qj = qn[kv * G:(kv + 1) * G, :]
        lsl = slice(kv * HD, (kv + 1) * HD)
        kc = kc_ref[0, :, lsl]
        vc = vc_ref[0, :, lsl]
        kn = kn_ref[0, :, lsl]
        vn = vn_ref[0, :, lsl]
        s_c = jnp.where(valid, _dot_nt(qj, kc), neg)
        s_n = jnp.sum(qj.astype(BF16).astype(F32) * kn.astype(BF16).astype(F32),
                      axis=-1, keepdims=True)
        sink = sink_ref[kv * G:(kv + 1) * G, :]
        m = jnp.maximum(jnp.maximum(jnp.max(s_c, axis=-1, keepdims=True), s_n), sink)
        p_c = jnp.exp(s_c - m)
        p_n = jnp.exp(s_n - m)
        denom = jnp.sum(p_c, axis=-1, keepdims=True) + p_n + jnp.exp(sink - m)
        p_c = p_c / denom
        p_n = p_n / denom
        o = _dot(p_c, vc) + p_n.astype(BF16).astype(F32) * vn.astype(BF16).astype(F32)
        o_ref[0, kv * G:(kv + 1) * G, :] = o.astype(o_ref.dtype)


def _swa_sample(q, k_buf, v_buf, k_new, v_new, q_norm_g, sinks):
    B, D = q.shape
    n_buf, KVW = k_buf.shape[1:]
    NH, HD = N_Q_HEADS, HEAD_DIM
    buf = pl.BlockSpec((1, n_buf, KVW), lambda b: (b, 0, 0))
    new = pl.BlockSpec((1, 1, KVW), lambda b: (b, 0, 0))
    o = pl.pallas_call(
        _swa_sample_body,
        grid=(B,),
        in_specs=[pl.BlockSpec((1, NH, HD), lambda b: (b, 0, 0)), buf, buf, new, new,
                  pl.BlockSpec((1, HD), lambda b: (0, 0)),
                  pl.BlockSpec((NH, 1), lambda b: (0, 0))],
        out_specs=pl.BlockSpec((1, NH, HD), lambda b: (b, 0, 0)),
        out_shape=jax.ShapeDtypeStruct((B, NH, HD), BF16),
        compiler_params=_cparams(("parallel",)),
        name="swa_sample",
    )(q.reshape(B, NH, HD), k_buf, v_buf, k_new, v_new,
      q_norm_g.astype(F32).reshape(1, HD), sinks.astype(F32).reshape(NH, 1))
    return o.reshape(B, D)


def _rms(x, g):
    return x * lax.rsqrt(jnp.mean(x * x, axis=-1, keepdims=True) + RMS_EPS) * g


def _silu(x):
    return x * jax.nn.sigmoid(x)


def _forward(x, mods, kvmod, wkv0, shift0, conv0, k_buf, v_buf, p, wsrc, tag):
    B, T, D = x.shape
    M = B * T
    depth = len(mods)
    n_a = depth // 2
    emit = T == 1
    wb = {}

    def mm(xin, wname, l=None, **kw):
        w = wsrc[wname] if l is None else wsrc[wname][l]
        out = _mm(xin, w, emit_w=emit, name=f"{tag}_{wname}", **kw)
        if emit:
            out, w16 = out
            wb[(wname, l)] = w16
        return out

    new_wkv, new_shift, new_conv = [], [], []
    k_att = v_att = k_win = v_win = None
    x2 = x.reshape(M, D)
    for l in range(depth):
        sh1, sc1, gt1, sh2, sc2, gt2 = jnp.split(mods[l], 6, axis=-1)
        x3 = x2.reshape(B, T, D)
        if l < n_a:
            mixes, h_last = _norm_shift(x3, p['ln1_g'][l], sc1, sh1, shift0[l], p['rwkv_mix'][l])
            xr, xw, xk, xv, xa, xg = [m.reshape(M, D) for m in mixes]
            r = mm(xr, 'rwkv_w_r', l)
            k = mm(xk, 'rwkv_w_k', l)
            v = mm(xv, 'rwkv_w_v', l)
            wl = mm(mm(xw, 'rwkv_w1', l, act="tanh", out_dtype=BF16), 'rwkv_w2', l)
            al = mm(mm(xa, 'rwkv_a1', l, out_dtype=BF16), 'rwkv_a2', l)
            g = mm(mm(xg, 'rwkv_g1', l, act="sigmoid", out_dtype=BF16), 'rwkv_g2', l)
            w0, a0 = p['rwkv_w0'][l], p['rwkv_a0'][l]
            k_k, k_a = p['rwkv_k_k'][l], p['rwkv_k_a'][l]
            r_k = p['rwkv_r_k'][l].reshape(D)
            lnx_w, lnx_b = p['rwkv_lnx_w'][l], p['rwkv_lnx_b'][l]
            if T > 1:
                sh = (B, T, D)
                yg, S = _rwkv_prompt(r.reshape(sh), k.reshape(sh), v.reshape(sh), wl.reshape(sh),
                                     al.reshape(sh), g.reshape(sh), w0, a0, k_k, k_a, r_k,
                                     lnx_w, lnx_b)
                yg = yg.reshape(M, D)
            else:
                Hh, N = RW_HEADS, RW_HEAD
                w_log = -jax.nn.softplus(-(w0 + wl)) - 0.5
                decay = jnp.exp(-jnp.exp(w_log))
                a = jax.nn.sigmoid(a0 + al)
                kk = (k * k_k).reshape(B, Hh, N)
                kk = kk / jnp.maximum(jnp.sqrt(jnp.sum(kk * kk, axis=-1, keepdims=True)), 1e-12)
                kk = kk.reshape(B, D)
                k2 = k * (1.0 + (a - 1.0) * k_a)
                S, y = _rwkv_step(wkv0[l].astype(F32), decay, kk, kk * a, k2, r, v)
                y4 = y.reshape(B, Hh, N)
                mu = jnp.mean(y4, axis=-1, keepdims=True)
                var = jnp.mean(jnp.square(y4 - mu), axis=-1, keepdims=True)
                y4 = ((y4 - mu) * lax.rsqrt(var + GN_EPS) * lnx_w.reshape(Hh, N)
                      + lnx_b.reshape(Hh, N))
                bonus = jnp.sum((r * k2 * r_k).reshape(B, Hh, N), axis=-1, keepdims=True)
                y4 = y4 + bonus * v.reshape(B, Hh, N)
                yg = (y4.reshape(B, D) * g).astype(BF16)
            new_wkv.append(S)
            new_shift.append(h_last)
            x2 = mm(yg, 'rwkv_w_o', l, res=x2, gate=gt1, rows_per_gate=T)
        else:
            j = l - n_a
            h = _norm_mod(x3, p['ln1_g'][l], sc1, sh1).reshape(M, D)
            q = mm(h, 'attn_w_q', j)
            if k_buf is None:
                attn = _swa_prompt(q.reshape(B, T, D), k_att, v_att, p['attn_q_norm_g'][j],
                                   p['attn_sinks'][j]).reshape(M, D)
            else:
                attn = _swa_sample(q, k_buf, v_buf, k_att, v_att, p['attn_q_norm_g'][j],
                                   p['attn_sinks'][j])
            x2 = mm(attn, 'attn_w_o', j, res=x2, gate=gt1, rows_per_gate=T)
        h2 = _norm_mod(x2.reshape(B, T, D), p['ln2_g'][l], sc2, sh2).reshape(M, D)
        if emit:
            act, cb, wg16, wu16 = _ffn_in_tok(h2, wsrc['ffn_w_in'][l], p['ffn_conv_w'][l],
                                              p['ffn_conv_b'][l], conv0[l])
            wb[('ffn_w_gate', l)] = wg16
            wb[('ffn_w_up', l)] = wu16
        else:
            act, cb = _ffn_in_seq(h2, wsrc['ffn_w_gate'][l], wsrc['ffn_w_up'][l],
                                  p['ffn_conv_w'][l], p['ffn_conv_b'][l], conv0[l], T)
        new_conv.append(cb)
        x2 = mm(act, 'ffn_w_out', l, res=x2, gate=gt2, rows_per_gate=T)
        if l == n_a - 1:
            sh, sc = jnp.split(kvmod, 2, axis=-1)
            hn = _norm_mod(x2.reshape(B, T, D), p['kv_norm_g'], sc, sh).reshape(M, D)
            kv = mm(hn, 'w_kv')
            KVW = kv.shape[-1] // 2
            k_new = _rms(kv[:, :KVW].reshape(B, T, N_KV_HEADS, HEAD_DIM), p['k_norm_g'])
            v_new = kv[:, KVW:].reshape(B, T, N_KV_HEADS, HEAD_DIM)
            k_att, v_att = k_new.reshape(B, T, KVW), v_new.reshape(B, T, KVW)
            if k_buf is None:
                w = min(WINDOW, T)
                k_win, v_win = k_new[:, T - w:], v_new[:, T - w:]
            else:
                n_buf = k_buf.shape[1]
                k_win = jnp.concatenate([k_buf, k_new], axis=1)[:, -n_buf:]
                v_win = jnp.concatenate([v_buf, v_new], axis=1)[:, -n_buf:]
                k_buf = k_buf.reshape(B, n_buf, KVW)
                v_buf = v_buf.reshape(B, n_buf, KVW)
    outs = (x2.reshape(B, T, D), jnp.stack(new_wkv), jnp.stack(new_shift), jnp.stack(new_conv),
            k_win, v_win)
    return outs, wb


class _Layered:
    def __init__(self, wb):
        self._wb = wb

    def __getitem__(self, name):
        if (name, None) in self._wb:
            return self._wb[(name, None)]
        n = 1 + max(l for (nm, l) in self._wb if nm == name)
        return [self._wb[(name, l)] for l in range(n)]


def kernel(x_prompt, x_sample, c_prompt, c_sample, state_wkv, state_shift, state_conv, cache_k_win, cache_v_win, mod_w, mod_b, ln1_g, ln2_g, rwkv_mix, rwkv_w0, rwkv_w1, rwkv_w2, rwkv_a0, rwkv_a1, rwkv_a2, rwkv_g1, rwkv_g2, rwkv_k_k, rwkv_k_a, rwkv_r_k, rwkv_w_r, rwkv_w_k, rwkv_w_v, rwkv_w_o, rwkv_lnx_w, rwkv_lnx_b, kv_norm_g, kv_mod_w, kv_mod_b, w_kv, k_norm_g, attn_w_q, attn_q_norm_g, attn_sinks, attn_w_o, ffn_w_in, ffn_conv_w, ffn_conv_b, ffn_w_out):
    p = dict(mod_w=mod_w, mod_b=mod_b, ln1_g=ln1_g, ln2_g=ln2_g,
             rwkv_mix=rwkv_mix, rwkv_w0=rwkv_w0, rwkv_w1=rwkv_w1, rwkv_w2=rwkv_w2,
             rwkv_a0=rwkv_a0, rwkv_a1=rwkv_a1, rwkv_a2=rwkv_a2, rwkv_g1=rwkv_g1, rwkv_g2=rwkv_g2,
             rwkv_k_k=rwkv_k_k, rwkv_k_a=rwkv_k_a, rwkv_r_k=rwkv_r_k, rwkv_w_r=rwkv_w_r,
             rwkv_w_k=rwkv_w_k, rwkv_w_v=rwkv_w_v, rwkv_w_o=rwkv_w_o,
             rwkv_lnx_w=rwkv_lnx_w, rwkv_lnx_b=rwkv_lnx_b,
             kv_norm_g=kv_norm_g, kv_mod_w=kv_mod_w, kv_mod_b=kv_mod_b, w_kv=w_kv, k_norm_g=k_norm_g,
             attn_w_q=attn_w_q, attn_q_norm_g=attn_q_norm_g, attn_sinks=attn_sinks, attn_w_o=attn_w_o,
             ffn_w_in=ffn_w_in, ffn_conv_w=ffn_conv_w, ffn_conv_b=ffn_conv_b, ffn_w_out=ffn_w_out)
    depth = mod_w.shape[0]
    n_a = depth // 2
    Bp = x_prompt.shape[0]
    dt = x_prompt.dtype

    c_all = _silu(jnp.concatenate([c_prompt, c_sample], axis=0)).astype(BF16)
    mods = [_mm(c_all, mod_w[l], bias=mod_b[l], name=f"mod{l}") for l in range(depth)]
    kvmod = _mm(c_all, kv_mod_w, bias=kv_mod_b, name="kvmod")

    outs_s, wb = _forward(x_sample, [m[Bp:] for m in mods], kvmod[Bp:], state_wkv, state_shift,
                          state_conv, cache_k_win, cache_v_win, p, p, "s")
    F = ffn_conv_b.shape[-1]
    wkv0 = jnp.zeros((n_a, Bp, RW_HEADS, RW_HEAD, RW_HEAD), dt)
    shift0 = jnp.zeros((n_a, Bp, D_MODEL), dt)
    conv0 = jnp.zeros((depth, Bp, CONV_W - 1, F), dt)
    outs_p, _ = _forward(x_prompt, [m[:Bp] for m in mods], kvmod[:Bp], wkv0, shift0, conv0,
                         None, None, p, _Layered(wb), "p")
    y_p, wkv_p, shift_p, conv_p, kwin_p, vwin_p = outs_p
    y_s, wkv_s, shift_s, conv_s, kwin_s, vwin_s = outs_s
    return (y_p, y_s, wkv_p, wkv_s, shift_p, shift_s, conv_p, conv_s,
            kwin_p, kwin_s, vwin_p, vwin_s)
```

```python
import functools

import jax
import jax.numpy as jnp
from jax import lax
from jax.experimental import pallas as pl
from jax.experimental.pallas import tpu as pltpu

F32 = jnp.float32
BF16 = jnp.bfloat16

D_MODEL = 4096
RW_HEAD = 64
RW_HEADS = D_MODEL // RW_HEAD
GN_EPS = 64e-5
RMS_EPS = 1e-6
HEAD_DIM = 64
N_Q_HEADS = D_MODEL // HEAD_DIM
N_KV_HEADS = N_Q_HEADS // 8
GQA_GROUP = N_Q_HEADS // N_KV_HEADS
WINDOW = 128
ATTN_SCALE = HEAD_DIM ** -0.5
CONV_W = 3

LANES = 128
SUBLANES = 8
MXU_COLS = 256
VMEM_LIMIT_BYTES = 56 * 1024 * 1024

RW_CHUNK = 64
RW_LANES_PER_STEP = 1024
NORM_ROWS = 256


def _cparams(sem):
    return pltpu.CompilerParams(dimension_semantics=sem, vmem_limit_bytes=VMEM_LIMIT_BYTES)


def _pick_tile(dim, prefs):
    for p in prefs:
        if dim >= p and dim % p == 0:
            return p
    return dim


def _sigmoid(x):
    return 1.0 / (1.0 + jnp.exp(-x))


def _mm_body(*refs, nk, has_bias, act, has_res, emit_w):
    x_ref, w_ref = refs[0], refs[1]
    pos = 2
    bias_ref = res_ref = gate_ref = wout_ref = None
    if has_bias:
        bias_ref = refs[pos]; pos += 1
    if has_res:
        res_ref, gate_ref = refs[pos], refs[pos + 1]; pos += 2
    o_ref = refs[pos]; pos += 1
    if emit_w:
        wout_ref = refs[pos]; pos += 1
    acc_ref = refs[pos] if nk > 1 else None

    w = w_ref[...].astype(BF16)
    if emit_w:
        wout_ref[...] = w
    part = jnp.dot(x_ref[...].astype(BF16), w, preferred_element_type=F32)

    def finish(acc):
        if has_bias:
            acc = acc + bias_ref[...]
        if act == "tanh":
            acc = jnp.tanh(acc)
        elif act == "sigmoid":
            acc = _sigmoid(acc)
        if has_res:
            acc = res_ref[...] + gate_ref[...] * acc
        o_ref[...] = acc.astype(o_ref.dtype)

    if nk == 1:
        finish(part)
    else:
        k = pl.program_id(2)

        @pl.when(k == 0)
        def _():
            acc_ref[...] = part

        @pl.when(k > 0)
        def _():
            acc_ref[...] += part

        @pl.when(k == nk - 1)
        def _():
            finish(acc_ref[...])


def _mm(x, w, *, layer=None, bias=None, act=None, res=None, gate=None, rows_per_gate=1,
        out_dtype=F32, emit_w=False, name="mm"):
    M, K = x.shape
    K2, N = w.shape[-2:]
    assert K == K2 and (w.ndim == 2) == (layer is None)
    tm = _pick_tile(M, (1024,))
    tn = _pick_tile(N, (1024, 512))
    tk = _pick_tile(K, (2048,)) if M < 1024 or K <= 2048 else (K if K <= 4096 else
                                                                _pick_tile(K, (3584, 2048)))
    nk = K // tk
    grid = (M // tm, N // tn, nk)
    assert not emit_w or grid[0] == 1

    if layer is None:
        w_spec = pl.BlockSpec((tk, tn), lambda i, j, k: (k, j))
    else:
        w_spec = pl.BlockSpec((None, tk, tn), lambda i, j, k: (layer, k, j))
    in_specs = [pl.BlockSpec((tm, tk), lambda i, j, k: (i, k)), w_spec]
    args = [x, w]
    if bias is not None:
        in_specs.append(pl.BlockSpec((1, tn), lambda i, j, k: (0, j)))
        args.append(bias.reshape(1, N).astype(F32))
    if res is not None:
        in_specs.append(pl.BlockSpec((tm, tn), lambda i, j, k: (i, j)))
        args.append(res)
        if rows_per_gate == 1:
            in_specs.append(pl.BlockSpec((tm, tn), lambda i, j, k: (i, j)))
            args.append(gate)
        else:
            assert rows_per_gate % tm == 0
            bpg = rows_per_gate // tm
            in_specs.append(pl.BlockSpec((None, 1, tn), lambda i, j, k: (i // bpg, 0, j)))
            args.append(gate.reshape(gate.shape[0], 1, N))
    out_specs = [pl.BlockSpec((tm, tn), lambda i, j, k: (i, j))]
    out_shape = [jax.ShapeDtypeStruct((M, N), out_dtype)]
    if emit_w:
        out_specs.append(pl.BlockSpec((tk, tn), lambda i, j, k: (k, j)))
        out_shape.append(jax.ShapeDtypeStruct((K, N), BF16))
    scratch = [pltpu.VMEM((tm, tn), F32)] if nk > 1 else []
    body = functools.partial(_mm_body, nk=nk, has_bias=bias is not None, act=act,
                             has_res=res is not None, emit_w=emit_w)
    outs = pl.pallas_call(
        body,
        grid=grid,
        in_specs=in_specs,
        out_specs=out_specs,
        out_shape=out_shape,
        scratch_shapes=scratch,
        compiler_params=_cparams(("parallel", "parallel", "arbitrary")),
        name=name,
    )(*args)
    return tuple(outs) if emit_w else outs[0]


def _accumulate(k, nk, pairs):
    if nk == 1:
        return

    @pl.when(k == 0)
    def _():
        for acc_ref, part in pairs:
            acc_ref[...] = part

    @pl.when(k > 0)
    def _():
        for acc_ref, part in pairs:
            acc_ref[...] += part


def _ffn_in_seq_body(x_ref, wg_ref, wu_ref, cw_ref, cb_ref, st_ref,
                     act_ref, st_out_ref, carry, *, tiles_per_seq, n_sub):
    i = pl.program_id(1)

    @pl.when(i % tiles_per_seq == 0)
    def _():
        carry[...] = st_ref[...]

    x = x_ref[...]
    tm = x.shape[0]
    ts = act_ref.shape[1] // n_sub
    row = lax.broadcasted_iota(jnp.int32, (SUBLANES, ts), 0)
    for s in range(n_sub):
        sl = slice(s * ts, (s + 1) * ts)
        g = jnp.dot(x, wg_ref[:, sl], preferred_element_type=F32)
        u = jnp.dot(x, wu_ref[:, sl], preferred_element_type=F32)
        tail = carry[:, sl]
        before2, before1 = tail[0:1], tail[1:2]
        r1 = pltpu.roll(g, 1, 0)
        r2 = pltpu.roll(g, 2, 0)
        head1 = jnp.where(row == 0, before1, r1[:SUBLANES])
        head2 = jnp.where(row == 0, before2, jnp.where(row == 1, before1, r2[:SUBLANES]))
        prev1 = jnp.concatenate([head1, r1[SUBLANES:]], axis=0)
        prev2 = jnp.concatenate([head2, r2[SUBLANES:]], axis=0)
        cw = cw_ref[:, sl]
        conv = cb_ref[:, sl] + cw[0:1] * prev2 + cw[1:2] * prev1 + cw[2:3] * g
        act_ref[:, sl] = (conv * _sigmoid(conv) * u).astype(act_ref.dtype)
        last = g[tm - (CONV_W - 1):, :]
        st_out_ref[:, sl] = last
        carry[:, sl] = last


def _ffn_in_seq(x, wg, wu, conv_w, conv_b, state, T):
    M, K = x.shape
    F = wg.shape[1]
    tm = _pick_tile(T, (1024,))
    tn = _pick_tile(F, (512,))
    tps = T // tm
    body = functools.partial(_ffn_in_seq_body, tiles_per_seq=tps, n_sub=tn // MXU_COLS)
    w_spec = pl.BlockSpec((K, tn), lambda j, i: (0, j))
    act, st = pl.pallas_call(
        body,
        grid=(F // tn, M // tm),
        in_specs=[pl.BlockSpec((tm, K), lambda j, i: (i, 0)),
                  w_spec, w_spec,
                  pl.BlockSpec((CONV_W, tn), lambda j, i: (0, j)),
                  pl.BlockSpec((1, tn), lambda j, i: (0, j)),
                  pl.BlockSpec((None, CONV_W - 1, tn), lambda j, i: (i // tps, 0, j))],
        out_specs=[pl.BlockSpec((tm, tn), lambda j, i: (i, j)),
                   pl.BlockSpec((None, CONV_W - 1, tn), lambda j, i: (i, 0, j))],
        out_shape=[jax.ShapeDtypeStruct((M, F), BF16),
                   jax.ShapeDtypeStruct((M // tm, CONV_W - 1, F), F32)],
        scratch_shapes=[pltpu.VMEM((CONV_W - 1, tn), F32)],
        compiler_params=_cparams(("parallel", "arbitrary")),
        name="ffn_in_seq",
    )(x, wg, wu, conv_w.astype(F32), conv_b.reshape(1, F).astype(F32), state.astype(F32))
    return act, st[tps - 1::tps]


def _ffn_in_tok_body(x_ref, wg_ref, wu_ref, cw_ref, cb_ref, s0_ref, s1_ref,
                     act_ref, g_ref, wgo_ref, wuo_ref, accg, accu, *, nk):
    k = pl.program_id(2)
    x = x_ref[...]
    wg = wg_ref[...].astype(BF16)
    wu = wu_ref[...].astype(BF16)
    wgo_ref[...] = wg
    wuo_ref[...] = wu
    pg = jnp.dot(x, wg, preferred_element_type=F32)
    pu = jnp.dot(x, wu, preferred_element_type=F32)
    _accumulate(k, nk, [(accg, pg), (accu, pu)])

    @pl.when(k == nk - 1)
    def _():
        g = accg[...] if nk > 1 else pg
        u = accu[...] if nk > 1 else pu
        cw = cw_ref[...]
        conv = cb_ref[...] + cw[0:1] * s0_ref[...] + cw[1:2] * s1_ref[...] + cw[2:3] * g
        act_ref[...] = (conv * _sigmoid(conv) * u).astype(act_ref.dtype)
        g_ref[...] = g


def _ffn_in_tok(x, w_in, layer, conv_w, conv_b, state):
    B, K = x.shape
    F = w_in.shape[-1] // 2
    tn = _pick_tile(F, (1024, 512))
    tk = _pick_tile(K, (2048,))
    nk = K // tk
    nj = F // tn
    body = functools.partial(_ffn_in_tok_body, nk=nk)
    row_spec = pl.BlockSpec((B, tn), lambda i, j, k: (0, j))
    wo_spec = pl.BlockSpec((tk, tn), lambda i, j, k: (k, j))
    s0 = state[:, 0, :].astype(F32)
    s1 = state[:, 1, :].astype(F32)
    act, g, wg, wu = pl.pallas_call(
        body,
        grid=(1, nj, nk),
        in_specs=[pl.BlockSpec((B, tk), lambda i, j, k: (0, k)),
                  pl.BlockSpec((None, tk, tn), lambda i, j, k: (layer, k, j)),
                  pl.BlockSpec((None, tk, tn), lambda i, j, k: (layer, k, j + nj)),
                  pl.BlockSpec((CONV_W, tn), lambda i, j, k: (0, j)),
                  pl.BlockSpec((1, tn), lambda i, j, k: (0, j)),
                  row_spec, row_spec],
        out_specs=[row_spec, row_spec, wo_spec, wo_spec],
        out_shape=[jax.ShapeDtypeStruct((B, F), BF16), jax.ShapeDtypeStruct((B, F), F32),
                   jax.ShapeDtypeStruct((K, F), BF16), jax.ShapeDtypeStruct((K, F), BF16)],
        scratch_shapes=[pltpu.VMEM((B, tn), F32), pltpu.VMEM((B, tn), F32)],
        compiler_params=_cparams(("parallel", "parallel", "arbitrary")),
        name="ffn_in_tok",
    )(x, w_in, w_in, conv_w.astype(F32), conv_b.reshape(1, F).astype(F32), s0, s1)
    return act, jnp.stack([s1, g], axis=1), wg, wu


def _norm_mod_rows(x, g, sc, sh):
    y = x * lax.rsqrt(jnp.mean(x * x, axis=-1, keepdims=True) + RMS_EPS) * g
    return y * (1.0 + sc) + sh


def _norm_body(x_ref, g_ref, sc_ref, sh_ref, o_ref):
    o_ref[0] = _norm_mod_rows(x_ref[0], g_ref[...], sc_ref[0], sh_ref[0]).astype(o_ref.dtype)


def _norm_mod(x, g, sc, sh):
    B, T, D = x.shape
    tr = _pick_tile(T, (NORM_ROWS,))
    tok = pl.BlockSpec((1, tr, D), lambda b, t: (b, t, 0))
    per_b = pl.BlockSpec((1, 1, D), lambda b, t: (b, 0, 0))
    return pl.pallas_call(
        _norm_body,
        grid=(B, T // tr),
        in_specs=[tok, pl.BlockSpec((1, D), lambda b, t: (0, 0)), per_b, per_b],
        out_specs=tok,
        out_shape=jax.ShapeDtypeStruct((B, T, D), BF16),
        compiler_params=_cparams(("parallel", "parallel")),
        name="norm_mod",
    )(x, g.reshape(1, D), sc.reshape(B, 1, D), sh.reshape(B, 1, D))


def _norm_shift_body(x_ref, g_ref, sc_ref, sh_ref, s0_ref, mix_ref, *rest, n_mix):
    outs = rest[:n_mix]
    last_ref = rest[n_mix]
    carry = rest[n_mix + 1]
    t = pl.program_id(1)

    @pl.when(t == 0)
    def _():
        carry[...] = s0_ref[0]

    h = _norm_mod_rows(x_ref[0], g_ref[...], sc_ref[0], sh_ref[0])
    tr = h.shape[0]
    if tr == 1:
        h_prev = carry[...]
    else:
        row = lax.broadcasted_iota(jnp.int32, h.shape, 0)
        h_prev = jnp.where(row == 0, carry[...], pltpu.roll(h, 1, 0))
    dx = h_prev - h
    mix = mix_ref[...]
    for i in range(n_mix):
        outs[i][0] = (h + dx * mix[i:i + 1]).astype(outs[i].dtype)
    last = h[tr - 1:tr, :]
    carry[...] = last
    last_ref[0] = last


def _norm_shift(x, g, sc, sh, shift_prev, mix):
    B, T, D = x.shape
    n_mix = mix.shape[0]
    tr = _pick_tile(T, (NORM_ROWS,))
    tok = pl.BlockSpec((1, tr, D), lambda b, t: (b, t, 0))
    per_b = pl.BlockSpec((1, 1, D), lambda b, t: (b, 0, 0))
    body = functools.partial(_norm_shift_body, n_mix=n_mix)
    outs = pl.pallas_call(
        body,
        grid=(B, T // tr),
        in_specs=[tok, pl.BlockSpec((1, D), lambda b, t: (0, 0)), per_b, per_b, per_b,
                  pl.BlockSpec((n_mix, D), lambda b, t: (0, 0))],
        out_specs=[tok] * n_mix + [per_b],
        out_shape=[jax.ShapeDtypeStruct((B, T, D), BF16)] * n_mix
        + [jax.ShapeDtypeStruct((B, 1, D), F32)],
        scratch_shapes=[pltpu.VMEM((1, D), F32)],
        compiler_params=_cparams(("parallel", "arbitrary")),
        name="norm_shift",
    )(x, g.reshape(1, D), sc.reshape(B, 1, D), sh.reshape(B, 1, D),
      shift_prev.astype(F32).reshape(B, 1, D), mix.astype(F32))
    return outs[:n_mix], outs[n_mix].reshape(B, D)


def _bdot(a, b):
    return lax.dot_general(a.astype(BF16), b.astype(BF16), (((2,), (1,)), ((0,), (0,))),
                           preferred_element_type=F32)


def _bdot_nt(a, b):
    return lax.dot_general(a.astype(BF16), b.astype(BF16), (((2,), (2,)), ((0,), (0,))),
                           preferred_element_type=F32)


def _bdot_tn(a, b):
    return lax.dot_general(a.astype(BF16), b.astype(BF16), (((1,), (1,)), ((0,), (0,))),
                           preferred_element_type=F32)


def _dot(a, b):
    return jnp.dot(a.astype(BF16), b.astype(BF16), preferred_element_type=F32)


def _dot_nt(a, b):
    return lax.dot_general(a.astype(BF16), b.astype(BF16), (((1,), (1,)), ((), ())),
                           preferred_element_type=F32)


def _head_sum(x, m0):
    s0 = jnp.sum(jnp.where(m0, x, 0.0), axis=-1, keepdims=True)
    s1 = jnp.sum(jnp.where(m0, 0.0, x), axis=-1, keepdims=True)
    return jnp.where(m0, s0, s1)


def _split_heads(x, m0):
    x0 = jnp.where(m0, x, 0.0)
    return jnp.concatenate([x0, x - x0], axis=-2)


def _rwkv_pairs(r, k, v, wl, al, g, w0, a0, k_k, k_a, r_k, lnx_w, lnx_b, S):
    C = RW_CHUNK
    H = RW_HEAD
    P = r.shape[0]
    lane = lax.broadcasted_iota(jnp.int32, (1, 1, LANES), 2)
    m0 = lane < H

    z = -(w0 + wl)
    softplus = jnp.maximum(z, 0.0) + jnp.log(1.0 + jnp.exp(-jnp.abs(z)))
    log_decay = -jnp.exp(-softplus - 0.5)
    a = 1.0 / (1.0 + jnp.exp(-(a0 + al)))
    kk = k * k_k
    kk = kk / jnp.maximum(jnp.sqrt(_head_sum(kk * kk, m0)), 1e-12)
    k2 = k * (1.0 + (a - 1.0) * k_a)
    b = kk * a

    ti = lax.broadcasted_iota(jnp.int32, (P, C, C), 1)
    si = lax.broadcasted_iota(jnp.int32, (P, C, C), 2)
    tri = jnp.where(ti >= si, 1.0, 0.0).astype(BF16)
    p_hi = log_decay.astype(BF16)
    rem = log_decay - p_hi.astype(F32)
    p_mid = rem.astype(BF16)
    p_lo = (rem - p_mid.astype(F32)).astype(BF16)
    cum3 = _bdot(tri, jnp.concatenate([p_hi, p_mid, p_lo], axis=2))
    cum = cum3[:, :, :LANES] + cum3[:, :, LANES:2 * LANES] + cum3[:, :, 2 * LANES:]

    mid = cum[:, C // 2 - 1:C // 2, :]
    last = cum[:, C - 1:C, :]
    e_fwd = jnp.exp(cum - mid)
    e_inv = jnp.exp(mid - cum)
    e_prev = jnp.exp(cum - log_decay - mid)
    d_mid = jnp.exp(mid)
    d_end_mid = jnp.exp(last - mid)
    d_end = jnp.exp(last)

    kk_t = kk * e_prev
    r_t = r * e_fwd
    b_t = b * e_inv
    k_t = k2 * e_inv
    q = jnp.concatenate([kk_t, r_t], axis=1)
    q0 = jnp.where(m0, q, 0.0)
    g0 = _bdot_nt(q0, jnp.concatenate([b_t, k_t], axis=1))
    g1 = _bdot_nt(q - q0, jnp.concatenate([k_t, b_t], axis=1))

    t_row = lax.broadcasted_iota(jnp.int32, (1, C, LANES), 1)
    s_col = lax.broadcasted_iota(jnp.int32, (1, C, LANES), 2) & (C - 1)
    strict = t_row > s_col
    incl = t_row >= s_col
    g0t, g0b, g1t, g1b = g0[:, :C], g0[:, C:], g1[:, :C], g1[:, C:]
    l_cat = jnp.where(strict, jnp.where(m0, g0t, g1t), 0.0)
    ak_cat = jnp.where(strict, jnp.where(m0, g1t, g0t), 0.0)
    gb0 = jnp.where(incl, g0b, 0.0)
    gb1 = jnp.where(incl, g1b, 0.0)
    grb_cat = jnp.where(m0, gb0, gb1)

    rr = lax.broadcasted_iota(jnp.int32, (1, LANES, LANES), 1)
    cc = lax.broadcasted_iota(jnp.int32, (1, LANES, LANES), 2)
    bd_mask = (rr < H) == (cc < H)

    def block_diag(x_cat):
        return jnp.where(bd_mask, jnp.concatenate([x_cat, x_cat], axis=1), 0.0)

    t_cat = jnp.where(t_row == s_col, 1.0, 0.0) - l_cat
    m_cat = l_cat
    n = 1
    while 2 * n < C:
        m_cat = _bdot(m_cat, block_diag(m_cat))
        t_cat = t_cat + _bdot(t_cat, block_diag(m_cat))
        n *= 2

    v0 = jnp.where(m0, v, 0.0)
    v1 = v - v0
    kk_h = _bdot(t_cat, _split_heads(kk_t, m0))
    av = _bdot(ak_cat, jnp.concatenate([v1, v0], axis=1))
    u0 = -_bdot(t_cat, _split_heads(av, m0))
    r_h = r_t - _bdot(grb_cat, _split_heads(kk_h, m0))
    u00 = jnp.where(m0, u0, 0.0)
    y0 = _bdot(jnp.concatenate([gb0, gb1], axis=2),
               jnp.concatenate([u00, v0, v1, u0 - u00], axis=1))
    b_end = b_t * d_end_mid
    k_end = k_t * d_end_mid
    s_add = jnp.where(bd_mask, _bdot_tn(jnp.concatenate([u0, v], axis=1),
                                        jnp.concatenate([b_end, k_end], axis=1)), 0.0)
    kb = jnp.where(bd_mask, _bdot_tn(kk_h, b_end), 0.0)

    s_mid = S * d_mid
    y = _bdot_nt(r_h, s_mid) + y0
    s_new = S * d_end - _bdot(s_mid, kb) + s_add

    inv_h = 1.0 / H
    mu = _head_sum(y, m0) * inv_h
    yc = y - mu
    var = _head_sum(yc * yc, m0) * inv_h
    yn = yc * lax.rsqrt(var + GN_EPS) * lnx_w + lnx_b
    yn = yn + _head_sum(r * k2 * r_k, m0) * v
    return yn * g, s_new


def _rwkv_chunk_body(r_ref, k_ref, v_ref, wl_ref, al_ref, g_ref,
                     w0_ref, a0_ref, kk_ref, ka_ref, rk_ref, lw_ref, lb_ref,
                     y_ref, s_out_ref, s_ref, *, n_pairs, n_chunks):
    c = pl.program_id(2)

    @pl.when(c == 0)
    def _():
        s_ref[...] = jnp.zeros_like(s_ref)

    def pairs(ref, lead):
        return jnp.stack([ref[lead + (slice(None), slice(p * LANES, (p + 1) * LANES))]
                          for p in range(n_pairs)], axis=0)

    tok = [pairs(ref, (0,)) for ref in (r_ref, k_ref, v_ref, wl_ref, al_ref, g_ref)]
    par = [pairs(ref, ()) for ref in (w0_ref, a0_ref, kk_ref, ka_ref, rk_ref, lw_ref, lb_ref)]
    yg, s_new = _rwkv_pairs(*tok, *par, s_ref[...])
    for p in range(n_pairs):
        y_ref[0, :, p * LANES:(p + 1) * LANES] = yg[p].astype(y_ref.dtype)
    s_ref[...] = s_new

    @pl.when(c == n_chunks - 1)
    def _():
        s_out_ref[0] = s_new


def _rwkv_prompt(r, k, v, wl, al, g, w0, a0, k_k, k_a, r_k, lnx_w, lnx_b):
    B, T, D = r.shape
    C, LW = RW_CHUNK, RW_LANES_PER_STEP
    n_pairs = LW // LANES
    n_chunks = T // C
    tok = pl.BlockSpec((1, C, LW), lambda b, h, c: (b, c, h))
    par = pl.BlockSpec((1, LW), lambda b, h, c: (0, h))
    body = functools.partial(_rwkv_chunk_body, n_pairs=n_pairs, n_chunks=n_chunks)
    y, s_bd = pl.pallas_call(
        body,
        grid=(B, D // LW, n_chunks),
        in_specs=[tok] * 6 + [par] * 7,
        out_specs=[pl.BlockSpec((1, C, LW), lambda b, h, c: (b, c, h)),
                   pl.BlockSpec((1, n_pairs, LANES, LANES), lambda b, h, c: (b, h, 0, 0))],
        out_shape=[jax.ShapeDtypeStruct((B, T, D), BF16),
                   jax.ShapeDtypeStruct((B, D // LANES, LANES, LANES), F32)],
        scratch_shapes=[pltpu.VMEM((n_pairs, LANES, LANES), F32)],
        compiler_params=_cparams(("parallel", "parallel", "arbitrary")),
        name="rwkv_chunk",
    )(r, k, v, wl, al, g, *[p.reshape(1, D) for p in (w0, a0, k_k, k_a, r_k, lnx_w, lnx_b)])
    H = RW_HEAD
    s = jnp.stack([s_bd[:, :, :H, :H], s_bd[:, :, H:, H:]], axis=2)
    return y, s.reshape(B, D // H, H, H)


def _rwkv_step_body(s_ref, d_ref, kk_ref, b_ref, k_ref, r_ref, v_ref, s_out_ref, y_ref):
    S = s_ref[0]
    s_kk = jnp.sum(S * kk_ref[0], axis=-1, keepdims=True)
    S = S * d_ref[0] - s_kk * b_ref[0] + v_ref[0] * k_ref[0]
    s_out_ref[0] = S
    y_ref[0] = jnp.sum(S * r_ref[0], axis=-1, keepdims=True)


def _rwkv_step(S0, decay, kk, b, k, r, v):
    B, Hh, N, _ = S0.shape
    row = lambda t: t.reshape(B, Hh, 1, N)
    col = lambda t: t.reshape(B, Hh, N, 1)
    row_spec = pl.BlockSpec((1, Hh, 1, N), lambda i: (i, 0, 0, 0))
    col_spec = pl.BlockSpec((1, Hh, N, 1), lambda i: (i, 0, 0, 0))
    st_spec = pl.BlockSpec((1, Hh, N, N), lambda i: (i, 0, 0, 0))
    S1, y = pl.pallas_call(
        _rwkv_step_body,
        grid=(B,),
        in_specs=[st_spec] + [row_spec] * 5 + [col_spec],
        out_specs=[st_spec, col_spec],
        out_shape=[jax.ShapeDtypeStruct(S0.shape, F32), jax.ShapeDtypeStruct((B, Hh, N, 1), F32)],
        compiler_params=_cparams(("parallel",)),
        name="rwkv_step",
    )(S0, row(decay), row(kk), row(b), row(k), row(r), col(v))
    return S1, y.reshape(B, Hh * N)


def _swa_prompt_body(q_ref, kp_ref, kc_ref, vp_ref, vc_ref, qg_ref, sink_ref, o_ref):
    i = pl.program_id(1)
    W = WINDOW
    lane = lax.broadcasted_iota(jnp.int32, (1, LANES), 1)
    m0 = lane < HEAD_DIM
    a_idx = lax.broadcasted_iota(jnp.int32, (W, 2 * W), 0)
    c_idx = lax.broadcasted_iota(jnp.int32, (W, 2 * W), 1)
    first_key = jnp.where(i > 0, 0, W)
    mask = (c_idx > a_idx) & (c_idx <= a_idx + W) & (c_idx >= first_key)
    neg = jnp.float32(-jnp.inf)
    qg = qg_ref[...]

    for blk in range(N_KV_HEADS // 2):
        ksl = slice(blk * LANES, (blk + 1) * LANES)
        kb = jnp.concatenate([kp_ref[0, :, ksl], kc_ref[0, :, ksl]], axis=0)
        vb = jnp.concatenate([vp_ref[0, :, ksl], vc_ref[0, :, ksl]], axis=0)
        kb_r = pltpu.roll(kb, HEAD_DIM, 1)
        vb_r = pltpu.roll(vb, HEAD_DIM, 1)
        for sub in range(2):
            if sub == 0:
                k2 = jnp.where(m0, kb, kb_r)
                v2 = jnp.where(m0, vb, vb_r)
            else:
                k2 = jnp.where(m0, kb_r, kb)
                v2 = jnp.where(m0, vb_r, vb)
            k2 = k2.astype(BF16)
            v2 = v2.astype(BF16)
            kv = 2 * blk + sub
            for pp in range(GQA_GROUP // 2):
                pair = kv * (GQA_GROUP // 2) + pp
                qsl = slice(pair * LANES, (pair + 1) * LANES)
                qv = q_ref[0, :, qsl]
                ms = _head_sum(qv * qv, m0) * (1.0 / HEAD_DIM)
                qn = qv * lax.rsqrt(ms + RMS_EPS) * qg * ATTN_SCALE
                sv = sink_ref[:, qsl]
                outs = []
                for h in range(2):
                    hm = m0 if h == 0 else jnp.logical_not(m0)
                    s = lax.dot_general(jnp.where(hm, qn, 0.0).astype(BF16), k2,
                                        (((1,), (1,)), ((), ())), preferred_element_type=F32)
                    s = jnp.where(mask, s, neg)
                    sink = jnp.max(jnp.where(hm, sv, neg), axis=-1, keepdims=True)
                    m = jnp.maximum(jnp.max(s, axis=-1, keepdims=True), sink)
                    p = jnp.exp(s - m)
                    denom = jnp.sum(p, axis=-1, keepdims=True) + jnp.exp(sink - m)
                    p = p / denom
                    outs.append(jnp.dot(p.astype(BF16), v2, preferred_element_type=F32))
                o_ref[0, :, qsl] = jnp.where(m0, outs[0], outs[1]).astype(o_ref.dtype)


def _swa_prompt(q, k, v, q_norm_g, sinks):
    B, T, D = q.shape
    KVW = k.shape[-1]
    W = WINDOW
    qg = jnp.tile(q_norm_g.astype(F32), 2).reshape(1, LANES)
    sink_l = jnp.repeat(sinks.astype(F32), HEAD_DIM).reshape(1, D)
    cur = pl.BlockSpec((1, W, KVW), lambda b, i: (b, i, 0))
    prev = pl.BlockSpec((1, W, KVW), lambda b, i: (b, jnp.maximum(i - 1, 0), 0))
    return pl.pallas_call(
        _swa_prompt_body,
        grid=(B, T // W),
        in_specs=[pl.BlockSpec((1, W, D), lambda b, i: (b, i, 0)), prev, cur, prev, cur,
                  pl.BlockSpec((1, LANES), lambda b, i: (0, 0)),
                  pl.BlockSpec((1, D), lambda b, i: (0, 0))],
        out_specs=pl.BlockSpec((1, W, D), lambda b, i: (b, i, 0)),
        out_shape=jax.ShapeDtypeStruct((B, T, D), BF16),
        compiler_params=_cparams(("parallel", "parallel")),
        name="swa_prompt",
    )(q, k, k, v, v, qg, sink_l)


def _swa_sample_body(q_ref, kc_ref, vc_ref, kn_ref, vn_ref, qg_ref, sink_ref, o_ref):
    G, HD = GQA_GROUP, HEAD_DIM
    q = q_ref[0]
    ms = jnp.mean(q * q, axis=-1, keepdims=True)
    qn = q * lax.rsqrt(ms + RMS_EPS) * qg_ref[...] * ATTN_SCALE
    n_buf = kc_ref.shape[1]
    col = lax.broadcasted_iota(jnp.int32, (G, n_buf), 1)
    valid = (n_buf - col) < WINDOW
    neg = jnp.float32(-jnp.inf)
    for kv in range(N_KV_HEADS):
        qj = qn[kv * G:(kv + 1) * G, :]
        lsl = slice(kv * HD, (kv + 1) * HD)
        kc = kc_ref[0, :, lsl]
        vc = vc_ref[0, :, lsl]
        kn = kn_ref[0, :, lsl]
        vn = vn_ref[0, :, lsl]
        s_c = jnp.where(valid, _dot_nt(qj, kc), neg)
        s_n = jnp.sum(qj.astype(BF16).astype(F32) * kn.astype(BF16).astype(F32),
                      axis=-1, keepdims=True)
        sink = sink_ref[kv * G:(kv + 1) * G, :]
        m = jnp.maximum(jnp.maximum(jnp.max(s_c, axis=-1, keepdims=True), s_n), sink)
        p_c = jnp.exp(s_c - m)
        p_n = jnp.exp(s_n - m)
        denom = jnp.sum(p_c, axis=-1, keepdims=True) + p_n + jnp.exp(sink - m)
        p_c = p_c / denom
        p_n = p_n / denom
        o = _dot(p_c, vc) + p_n.astype(BF16).astype(F32) * vn.astype(BF16).astype(F32)
        o_ref[0, kv * G:(kv + 1) * G, :] = o.astype(o_ref.dtype)


def _swa_sample(q, k_buf, v_buf, k_new, v_new, q_norm_g, sinks):
    B, D = q.shape
    n_buf, KVW = k_buf.shape[1:]
    NH, HD = N_Q_HEADS, HEAD_DIM
    buf = pl.BlockSpec((1, n_buf, KVW), lambda b: (b, 0, 0))
    new = pl.BlockSpec((1, 1, KVW), lambda b: (b, 0, 0))
    o = pl.pallas_call(
        _swa_sample_body,
        grid=(B,),
        in_specs=[pl.BlockSpec((1, NH, HD), lambda b: (b, 0, 0)), buf, buf, new, new,
                  pl.BlockSpec((1, HD), lambda b: (0, 0)),
                  pl.BlockSpec((NH, 1), lambda b: (0, 0))],
        out_specs=pl.BlockSpec((1, NH, HD), lambda b: (b, 0, 0)),
        out_shape=jax.ShapeDtypeStruct((B, NH, HD), BF16),
        compiler_params=_cparams(("parallel",)),
        name="swa_sample",
    )(q.reshape(B, NH, HD), k_buf, v_buf, k_new, v_new,
      q_norm_g.astype(F32).reshape(1, HD), sinks.astype(F32).reshape(NH, 1))
    return o.reshape(B, D)


def _rms(x, g):
    return x * lax.rsqrt(jnp.mean(x * x, axis=-1, keepdims=True) + RMS_EPS) * g


def _silu(x):
    return x * jax.nn.sigmoid(x)


def _forward(x, mods, kvmod, wkv0, shift0, conv0, k_buf, v_buf, p, wsrc, tag):
    B, T, D = x.shape
    M = B * T
    depth = len(mods)
    n_a = depth // 2
    emit = T == 1
    wb = {}

    def mm(xin, wname, l=None, **kw):
        if emit:
            out, w16 = _mm(xin, wsrc[wname], layer=l, emit_w=True, name=f"{tag}_{wname}", **kw)
            wb[(wname, l)] = w16
            return out
        w = wsrc[wname] if l is None else wsrc[wname][l]
        return _mm(xin, w, name=f"{tag}_{wname}", **kw)

    new_wkv, new_shift, new_conv = [], [], []
    k_att = v_att = k_win = v_win = None
    x2 = x.reshape(M, D)
    for l in range(depth):
        sh1, sc1, gt1, sh2, sc2, gt2 = jnp.split(mods[l], 6, axis=-1)
        x3 = x2.reshape(B, T, D)
        if l < n_a:
            mixes, h_last = _norm_shift(x3, p['ln1_g'][l], sc1, sh1, shift0[l], p['rwkv_mix'][l])
            xr, xw, xk, xv, xa, xg = [m.reshape(M, D) for m in mixes]
            r = mm(xr, 'rwkv_w_r', l)
            k = mm(xk, 'rwkv_w_k', l)
            v = mm(xv, 'rwkv_w_v', l)
            wl = mm(mm(xw, 'rwkv_w1', l, act="tanh", out_dtype=BF16), 'rwkv_w2', l)
            al = mm(mm(xa, 'rwkv_a1', l, out_dtype=BF16), 'rwkv_a2', l)
            g = mm(mm(xg, 'rwkv_g1', l, act="sigmoid", out_dtype=BF16), 'rwkv_g2', l)
            w0, a0 = p['rwkv_w0'][l], p['rwkv_a0'][l]
            k_k, k_a = p['rwkv_k_k'][l], p['rwkv_k_a'][l]
            r_k = p['rwkv_r_k'][l].reshape(D)
            lnx_w, lnx_b = p['rwkv_lnx_w'][l], p['rwkv_lnx_b'][l]
            if T > 1:
                sh = (B, T, D)
                yg, S = _rwkv_prompt(r.reshape(sh), k.reshape(sh), v.reshape(sh), wl.reshape(sh),
                                     al.reshape(sh), g.reshape(sh), w0, a0, k_k, k_a, r_k,
                                     lnx_w, lnx_b)
                yg = yg.reshape(M, D)
            else:
                Hh, N = RW_HEADS, RW_HEAD
                w_log = -jax.nn.softplus(-(w0 + wl)) - 0.5
                decay = jnp.exp(-jnp.exp(w_log))
                a = jax.nn.sigmoid(a0 + al)
                kk = (k * k_k).reshape(B, Hh, N)
                kk = kk / jnp.maximum(jnp.sqrt(jnp.sum(kk * kk, axis=-1, keepdims=True)), 1e-12)
                kk = kk.reshape(B, D)
                k2 = k * (1.0 + (a - 1.0) * k_a)
                S, y = _rwkv_step(wkv0[l].astype(F32), decay, kk, kk * a, k2, r, v)
                y4 = y.reshape(B, Hh, N)
                mu = jnp.mean(y4, axis=-1, keepdims=True)
                var = jnp.mean(jnp.square(y4 - mu), axis=-1, keepdims=True)
                y4 = ((y4 - mu) * lax.rsqrt(var + GN_EPS) * lnx_w.reshape(Hh, N)
                      + lnx_b.reshape(Hh, N))
                bonus = jnp.sum((r * k2 * r_k).reshape(B, Hh, N), axis=-1, keepdims=True)
                y4 = y4 + bonus * v.reshape(B, Hh, N)
                yg = (y4.reshape(B, D) * g).astype(BF16)
            new_wkv.append(S)
            new_shift.append(h_last)
            x2 = mm(yg, 'rwkv_w_o', l, res=x2, gate=gt1, rows_per_gate=T)
        else:
            j = l - n_a
            h = _norm_mod(x3, p['ln1_g'][l], sc1, sh1).reshape(M, D)
            q = mm(h, 'attn_w_q', j)
            if k_buf is None:
                attn = _swa_prompt(q.reshape(B, T, D), k_att, v_att, p['attn_q_norm_g'][j],
                                   p['attn_sinks'][j]).reshape(M, D)
            else:
                attn = _swa_sample(q, k_buf, v_buf, k_att, v_att, p['attn_q_norm_g'][j],
                                   p['attn_sinks'][j])
            x2 = mm(attn, 'attn_w_o', j, res=x2, gate=gt1, rows_per_gate=T)
        h2 = _norm_mod(x2.reshape(B, T, D), p['ln2_g'][l], sc2, sh2).reshape(M, D)
        if emit:
            act, cb, wg16, wu16 = _ffn_in_tok(h2, wsrc['ffn_w_in'], l, p['ffn_conv_w'][l],
                                              p['ffn_conv_b'][l], conv0[l])
            wb[('ffn_w_gate', l)] = wg16
            wb[('ffn_w_up', l)] = wu16
        else:
            act, cb = _ffn_in_seq(h2, wsrc['ffn_w_gate'][l], wsrc['ffn_w_up'][l],
                                  p['ffn_conv_w'][l], p['ffn_conv_b'][l], conv0[l], T)
        new_conv.append(cb)
        x2 = mm(act, 'ffn_w_out', l, res=x2, gate=gt2, rows_per_gate=T)
        if l == n_a - 1:
            sh, sc = jnp.split(kvmod, 2, axis=-1)
            hn = _norm_mod(x2.reshape(B, T, D), p['kv_norm_g'], sc, sh).reshape(M, D)
            kv = mm(hn, 'w_kv')
            KVW = kv.shape[-1] // 2
            k_new = _rms(kv[:, :KVW].reshape(B, T, N_KV_HEADS, HEAD_DIM), p['k_norm_g'])
            v_new = kv[:, KVW:].reshape(B, T, N_KV_HEADS, HEAD_DIM)
            k_att, v_att = k_new.reshape(B, T, KVW), v_new.reshape(B, T, KVW)
            if k_buf is None:
                w = min(WINDOW, T)
                k_win, v_win = k_new[:, T - w:], v_new[:, T - w:]
            else:
                n_buf = k_buf.shape[1]
                k_win = jnp.concatenate([k_buf, k_new], axis=1)[:, -n_buf:]
                v_win = jnp.concatenate([v_buf, v_new], axis=1)[:, -n_buf:]
                k_buf = k_buf.reshape(B, n_buf, KVW)
                v_buf = v_buf.reshape(B, n_buf, KVW)
    outs = (x2.reshape(B, T, D), jnp.stack(new_wkv), jnp.stack(new_shift), jnp.stack(new_conv),
            k_win, v_win)
    return outs, wb


class _Layered:
    def __init__(self, wb):
        self._wb = wb

    def __getitem__(self, name):
        if (name, None) in self._wb:
            return self._wb[(name, None)]
        n = 1 + max(l for (nm, l) in self._wb if nm == name)
        return [self._wb[(name, l)] for l in range(n)]


def kernel(x_prompt, x_sample, c_prompt, c_sample, state_wkv, state_shift, state_conv, cache_k_win, cache_v_win, mod_w, mod_b, ln1_g, ln2_g, rwkv_mix, rwkv_w0, rwkv_w1, rwkv_w2, rwkv_a0, rwkv_a1, rwkv_a2, rwkv_g1, rwkv_g2, rwkv_k_k, rwkv_k_a, rwkv_r_k, rwkv_w_r, rwkv_w_k, rwkv_w_v, rwkv_w_o, rwkv_lnx_w, rwkv_lnx_b, kv_norm_g, kv_mod_w, kv_mod_b, w_kv, k_norm_g, attn_w_q, attn_q_norm_g, attn_sinks, attn_w_o, ffn_w_in, ffn_conv_w, ffn_conv_b, ffn_w_out):
    p = dict(mod_w=mod_w, mod_b=mod_b, ln1_g=ln1_g, ln2_g=ln2_g,
             rwkv_mix=rwkv_mix, rwkv_w0=rwkv_w0, rwkv_w1=rwkv_w1, rwkv_w2=rwkv_w2,
             rwkv_a0=rwkv_a0, rwkv_a1=rwkv_a1, rwkv_a2=rwkv_a2, rwkv_g1=rwkv_g1, rwkv_g2=rwkv_g2,
             rwkv_k_k=rwkv_k_k, rwkv_k_a=rwkv_k_a, rwkv_r_k=rwkv_r_k, rwkv_w_r=rwkv_w_r,
             rwkv_w_k=rwkv_w_k, rwkv_w_v=rwkv_w_v, rwkv_w_o=rwkv_w_o,
             rwkv_lnx_w=rwkv_lnx_w, rwkv_lnx_b=rwkv_lnx_b,
             kv_norm_g=kv_norm_g, kv_mod_w=kv_mod_w, kv_mod_b=kv_mod_b, w_kv=w_kv, k_norm_g=k_norm_g,
             attn_w_q=attn_w_q, attn_q_norm_g=attn_q_norm_g, attn_sinks=attn_sinks, attn_w_o=attn_w_o,
             ffn_w_in=ffn_w_in, ffn_conv_w=ffn_conv_w, ffn_conv_b=ffn_conv_b, ffn_w_out=ffn_w_out)
    depth = mod_w.shape[0]
    n_a = depth // 2
    Bp = x_prompt.shape[0]
    dt = x_prompt.dtype

    c_all = _silu(jnp.concatenate([c_prompt, c_sample], axis=0)).astype(BF16)
    mods = [_mm(c_all, mod_w, layer=l, bias=mod_b[l], name=f"mod{l}") for l in range(depth)]
    kvmod = _mm(c_all, kv_mod_w, bias=kv_mod_b, name="kvmod")

    outs_s, wb = _forward(x_sample, [m[Bp:] for m in mods], kvmod[Bp:], state_wkv, state_shift,
                          state_conv, cache_k_win, cache_v_win, p, p, "s")
    F = ffn_conv_b.shape[-1]
    wkv0 = jnp.zeros((n_a, Bp, RW_HEADS, RW_HEAD, RW_HEAD), dt)
    shift0 = jnp.zeros((n_a, Bp, D_MODEL), dt)
    conv0 = jnp.zeros((depth, Bp, CONV_W - 1, F), dt)
    outs_p, _ = _forward(x_prompt, [m[:Bp] for m in mods], kvmod[:Bp], wkv0, shift0, conv0,
                         None, None, p, _Layered(wb), "p")
    y_p, wkv_p, shift_p, conv_p, kwin_p, vwin_p = outs_p
    y_s, wkv_s, shift_s, conv_s, kwin_s, vwin_s = outs_s
    return (y_p, y_s, wkv_p, wkv_s, shift_p, shift_s, conv_p, conv_s,
            kwin_p, kwin_s, vwin_p, vwin_s)
```

```python
import functools

import jax
import jax.numpy as jnp
from jax import lax
from jax.experimental import pallas as pl
from jax.experimental.pallas import tpu as pltpu

F32 = jnp.float32
BF16 = jnp.bfloat16

D_MODEL = 4096
RW_HEAD = 64
RW_HEADS = D_MODEL // RW_HEAD
GN_EPS = 64e-5
DECAY_SCALE = 0.6065306597126334
RMS_EPS = 1e-6
HEAD_DIM = 64
N_Q_HEADS = D_MODEL // HEAD_DIM
N_KV_HEADS = N_Q_HEADS // 8
GQA_GROUP = N_Q_HEADS // N_KV_HEADS
WINDOW = 128
ATTN_SCALE = HEAD_DIM ** -0.5
CONV_W = 3

LANES = 128
SUBLANES = 8
MXU_COLS = 256
VMEM_LIMIT_BYTES = 56 * 1024 * 1024

RW_CHUNK = 64
RW_LANES_PER_STEP = 2048
RW_SUBCHUNKS = 2
NORM_ROWS = 256


def _cparams(sem):
    return pltpu.CompilerParams(dimension_semantics=sem, vmem_limit_bytes=VMEM_LIMIT_BYTES)


def _pick_tile(dim, prefs):
    for p in prefs:
        if dim >= p and dim % p == 0:
            return p
    return dim


def _sigmoid(x):
    return 1.0 / (1.0 + jnp.exp(-x))


def _mm_body(*refs, nk, has_bias, act, has_res, emit_w):
    x_ref, w_ref = refs[0], refs[1]
    pos = 2
    bias_ref = res_ref = gate_ref = wout_ref = None
    if has_bias:
        bias_ref = refs[pos]; pos += 1
    if has_res:
        res_ref, gate_ref = refs[pos], refs[pos + 1]; pos += 2
    o_ref = refs[pos]; pos += 1
    if emit_w:
        wout_ref = refs[pos]; pos += 1
    acc_ref = refs[pos] if nk > 1 else None

    w = w_ref[...].astype(BF16)
    if emit_w:
        wout_ref[...] = w
    part = jnp.dot(x_ref[...].astype(BF16), w, preferred_element_type=F32)

    def finish(acc):
        if has_bias:
            acc = acc + bias_ref[...]
        if act == "tanh":
            acc = jnp.tanh(acc)
        elif act == "sigmoid":
            acc = _sigmoid(acc)
        if has_res:
            acc = res_ref[...] + gate_ref[...] * acc
        o_ref[...] = acc.astype(o_ref.dtype)

    if nk == 1:
        finish(part)
    else:
        k = pl.program_id(2)

        @pl.when(k == 0)
        def _():
            acc_ref[...] = part

        @pl.when(k > 0)
        def _():
            acc_ref[...] += part

        @pl.when(k == nk - 1)
        def _():
            finish(acc_ref[...])


def _mm(x, w, *, layer=None, bias=None, act=None, res=None, gate=None, rows_per_gate=1,
        out_dtype=F32, emit_w=False, name="mm"):
    M, K = x.shape
    K2, N = w.shape[-2:]
    assert K == K2 and (w.ndim == 2) == (layer is None)
    tm = _pick_tile(M, (1024,))
    tn = _pick_tile(N, (1024, 512))
    tk = _pick_tile(K, (2048,)) if M < 1024 or K <= 2048 else (K if K <= 4096 else
                                                                _pick_tile(K, (3584, 2048)))
    nk = K // tk
    grid = (M // tm, N // tn, nk)
    assert not emit_w or grid[0] == 1

    if layer is None:
        w_spec = pl.BlockSpec((tk, tn), lambda i, j, k: (k, j))
    else:
        w_spec = pl.BlockSpec((None, tk, tn), lambda i, j, k: (layer, k, j))
    in_specs = [pl.BlockSpec((tm, tk), lambda i, j, k: (i, k)), w_spec]
    args = [x, w]
    if bias is not None:
        in_specs.append(pl.BlockSpec((1, tn), lambda i, j, k: (0, j)))
        args.append(bias.reshape(1, N).astype(F32))
    if res is not None:
        in_specs.append(pl.BlockSpec((tm, tn), lambda i, j, k: (i, j)))
        args.append(res)
        if rows_per_gate == 1:
            in_specs.append(pl.BlockSpec((tm, tn), lambda i, j, k: (i, j)))
            args.append(gate)
        else:
            assert rows_per_gate % tm == 0
            bpg = rows_per_gate // tm
            in_specs.append(pl.BlockSpec((None, 1, tn), lambda i, j, k: (i // bpg, 0, j)))
            args.append(gate.reshape(gate.shape[0], 1, N))
    out_specs = [pl.BlockSpec((tm, tn), lambda i, j, k: (i, j))]
    out_shape = [jax.ShapeDtypeStruct((M, N), out_dtype)]
    if emit_w:
        out_specs.append(pl.BlockSpec((tk, tn), lambda i, j, k: (k, j)))
        out_shape.append(jax.ShapeDtypeStruct((K, N), BF16))
    scratch = [pltpu.VMEM((tm, tn), F32)] if nk > 1 else []
    body = functools.partial(_mm_body, nk=nk, has_bias=bias is not None, act=act,
                             has_res=res is not None, emit_w=emit_w)
    outs = pl.pallas_call(
        body,
        grid=grid,
        in_specs=in_specs,
        out_specs=out_specs,
        out_shape=out_shape,
        scratch_shapes=scratch,
        compiler_params=_cparams(("parallel", "parallel", "arbitrary")),
        name=name,
    )(*args)
    return tuple(outs) if emit_w else outs[0]


def _accumulate(k, nk, pairs):
    if nk == 1:
        return

    @pl.when(k == 0)
    def _():
        for acc_ref, part in pairs:
            acc_ref[...] = part

    @pl.when(k > 0)
    def _():
        for acc_ref, part in pairs:
            acc_ref[...] += part


def _ffn_in_seq_body(x_ref, wg_ref, wu_ref, cw_ref, cb_ref, st_ref,
                     act_ref, st_out_ref, carry, *, tiles_per_seq, n_sub):
    i = pl.program_id(1)

    @pl.when(i % tiles_per_seq == 0)
    def _():
        carry[...] = st_ref[...]

    x = x_ref[...]
    tm = x.shape[0]
    ts = act_ref.shape[1] // n_sub
    row = lax.broadcasted_iota(jnp.int32, (SUBLANES, ts), 0)
    for s in range(n_sub):
        sl = slice(s * ts, (s + 1) * ts)
        g = jnp.dot(x, wg_ref[:, sl], preferred_element_type=F32)
        u = jnp.dot(x, wu_ref[:, sl], preferred_element_type=F32)
        tail = carry[:, sl]
        before2, before1 = tail[0:1], tail[1:2]
        r1 = pltpu.roll(g, 1, 0)
        r2 = pltpu.roll(g, 2, 0)
        head1 = jnp.where(row == 0, before1, r1[:SUBLANES])
        head2 = jnp.where(row == 0, before2, jnp.where(row == 1, before1, r2[:SUBLANES]))
        prev1 = jnp.concatenate([head1, r1[SUBLANES:]], axis=0)
        prev2 = jnp.concatenate([head2, r2[SUBLANES:]], axis=0)
        cw = cw_ref[:, sl]
        conv = cb_ref[:, sl] + cw[0:1] * prev2 + cw[1:2] * prev1 + cw[2:3] * g
        act_ref[:, sl] = (conv * _sigmoid(conv) * u).astype(act_ref.dtype)
        last = g[tm - (CONV_W - 1):, :]
        st_out_ref[:, sl] = last
        carry[:, sl] = last


def _ffn_in_seq(x, wg, wu, conv_w, conv_b, state, T):
    M, K = x.shape
    F = wg.shape[1]
    tm = _pick_tile(T, (1024,))
    tn = _pick_tile(F, (512,))
    tps = T // tm
    body = functools.partial(_ffn_in_seq_body, tiles_per_seq=tps, n_sub=tn // MXU_COLS)
    w_spec = pl.BlockSpec((K, tn), lambda j, i: (0, j))
    act, st = pl.pallas_call(
        body,
        grid=(F // tn, M // tm),
        in_specs=[pl.BlockSpec((tm, K), lambda j, i: (i, 0)),
                  w_spec, w_spec,
                  pl.BlockSpec((CONV_W, tn), lambda j, i: (0, j)),
                  pl.BlockSpec((1, tn), lambda j, i: (0, j)),
                  pl.BlockSpec((None, CONV_W - 1, tn), lambda j, i: (i // tps, 0, j))],
        out_specs=[pl.BlockSpec((tm, tn), lambda j, i: (i, j)),
                   pl.BlockSpec((None, CONV_W - 1, tn), lambda j, i: (i, 0, j))],
        out_shape=[jax.ShapeDtypeStruct((M, F), BF16),
                   jax.ShapeDtypeStruct((M // tm, CONV_W - 1, F), F32)],
        scratch_shapes=[pltpu.VMEM((CONV_W - 1, tn), F32)],
        compiler_params=_cparams(("parallel", "arbitrary")),
        name="ffn_in_seq",
    )(x, wg, wu, conv_w.astype(F32), conv_b.reshape(1, F).astype(F32), state.astype(F32))
    return act, st[tps - 1::tps]


def _ffn_in_tok_body(x_ref, wg_ref, wu_ref, cw_ref, cb_ref, s0_ref, s1_ref,
                     act_ref, g_ref, wgo_ref, wuo_ref, accg, accu, *, nk):
    k = pl.program_id(2)
    x = x_ref[...]
    wg = wg_ref[...].astype(BF16)
    wu = wu_ref[...].astype(BF16)
    wgo_ref[...] = wg
    wuo_ref[...] = wu
    pg = jnp.dot(x, wg, preferred_element_type=F32)
    pu = jnp.dot(x, wu, preferred_element_type=F32)
    _accumulate(k, nk, [(accg, pg), (accu, pu)])

    @pl.when(k == nk - 1)
    def _():
        g = accg[...] if nk > 1 else pg
        u = accu[...] if nk > 1 else pu
        cw = cw_ref[...]
        conv = cb_ref[...] + cw[0:1] * s0_ref[...] + cw[1:2] * s1_ref[...] + cw[2:3] * g
        act_ref[...] = (conv * _sigmoid(conv) * u).astype(act_ref.dtype)
        g_ref[...] = g


def _ffn_in_tok(x, w_in, layer, conv_w, conv_b, state):
    B, K = x.shape
    F = w_in.shape[-1] // 2
    tn = _pick_tile(F, (1024, 512))
    tk = _pick_tile(K, (2048,))
    nk = K // tk
    nj = F // tn
    body = functools.partial(_ffn_in_tok_body, nk=nk)
    row_spec = pl.BlockSpec((B, tn), lambda i, j, k: (0, j))
    wo_spec = pl.BlockSpec((tk, tn), lambda i, j, k: (k, j))
    s0 = state[:, 0, :].astype(F32)
    s1 = state[:, 1, :].astype(F32)
    act, g, wg, wu = pl.pallas_call(
        body,
        grid=(1, nj, nk),
        in_specs=[pl.BlockSpec((B, tk), lambda i, j, k: (0, k)),
                  pl.BlockSpec((None, tk, tn), lambda i, j, k: (layer, k, j)),
                  pl.BlockSpec((None, tk, tn), lambda i, j, k: (layer, k, j + nj)),
                  pl.BlockSpec((CONV_W, tn), lambda i, j, k: (0, j)),
                  pl.BlockSpec((1, tn), lambda i, j, k: (0, j)),
                  row_spec, row_spec],
        out_specs=[row_spec, row_spec, wo_spec, wo_spec],
        out_shape=[jax.ShapeDtypeStruct((B, F), BF16), jax.ShapeDtypeStruct((B, F), F32),
                   jax.ShapeDtypeStruct((K, F), BF16), jax.ShapeDtypeStruct((K, F), BF16)],
        scratch_shapes=[pltpu.VMEM((B, tn), F32), pltpu.VMEM((B, tn), F32)],
        compiler_params=_cparams(("parallel", "parallel", "arbitrary")),
        name="ffn_in_tok",
    )(x, w_in, w_in, conv_w.astype(F32), conv_b.reshape(1, F).astype(F32), s0, s1)
    return act, jnp.stack([s1, g], axis=1), wg, wu


def _norm_mod_rows(x, g, sc, sh):
    y = x * lax.rsqrt(jnp.mean(x * x, axis=-1, keepdims=True) + RMS_EPS) * g
    return y * (1.0 + sc) + sh


def _norm_body(x_ref, g_ref, sc_ref, sh_ref, o_ref):
    o_ref[0] = _norm_mod_rows(x_ref[0], g_ref[...], sc_ref[0], sh_ref[0]).astype(o_ref.dtype)


def _norm_mod(x, g, sc, sh):
    B, T, D = x.shape
    tr = _pick_tile(T, (NORM_ROWS,))
    tok = pl.BlockSpec((1, tr, D), lambda b, t: (b, t, 0))
    per_b = pl.BlockSpec((1, 1, D), lambda b, t: (b, 0, 0))
    return pl.pallas_call(
        _norm_body,
        grid=(B, T // tr),
        in_specs=[tok, pl.BlockSpec((1, D), lambda b, t: (0, 0)), per_b, per_b],
        out_specs=tok,
        out_shape=jax.ShapeDtypeStruct((B, T, D), BF16),
        compiler_params=_cparams(("parallel", "parallel")),
        name="norm_mod",
    )(x, g.reshape(1, D), sc.reshape(B, 1, D), sh.reshape(B, 1, D))


def _norm_shift_body(x_ref, g_ref, sc_ref, sh_ref, s0_ref, mix_ref, *rest, n_mix):
    outs = rest[:n_mix]
    last_ref = rest[n_mix]
    carry = rest[n_mix + 1]
    t = pl.program_id(1)

    @pl.when(t == 0)
    def _():
        carry[...] = s0_ref[0]

    h = _norm_mod_rows(x_ref[0], g_ref[...], sc_ref[0], sh_ref[0])
    tr = h.shape[0]
    if tr == 1:
        h_prev = carry[...]
    else:
        row = lax.broadcasted_iota(jnp.int32, h.shape, 0)
        h_prev = jnp.where(row == 0, carry[...], pltpu.roll(h, 1, 0))
    dx = h_prev - h
    mix = mix_ref[...]
    for i in range(n_mix):
        outs[i][0] = (h + dx * mix[i:i + 1]).astype(outs[i].dtype)
    last = h[tr - 1:tr, :]
    carry[...] = last
    last_ref[0] = last


def _norm_shift(x, g, sc, sh, shift_prev, mix):
    B, T, D = x.shape
    n_mix = mix.shape[0]
    tr = _pick_tile(T, (NORM_ROWS,))
    tok = pl.BlockSpec((1, tr, D), lambda b, t: (b, t, 0))
    per_b = pl.BlockSpec((1, 1, D), lambda b, t: (b, 0, 0))
    body = functools.partial(_norm_shift_body, n_mix=n_mix)
    outs = pl.pallas_call(
        body,
        grid=(B, T // tr),
        in_specs=[tok, pl.BlockSpec((1, D), lambda b, t: (0, 0)), per_b, per_b, per_b,
                  pl.BlockSpec((n_mix, D), lambda b, t: (0, 0))],
        out_specs=[tok] * n_mix + [per_b],
        out_shape=[jax.ShapeDtypeStruct((B, T, D), BF16)] * n_mix
        + [jax.ShapeDtypeStruct((B, 1, D), F32)],
        scratch_shapes=[pltpu.VMEM((1, D), F32)],
        compiler_params=_cparams(("parallel", "arbitrary")),
        name="norm_shift",
    )(x, g.reshape(1, D), sc.reshape(B, 1, D), sh.reshape(B, 1, D),
      shift_prev.astype(F32).reshape(B, 1, D), mix.astype(F32))
    return outs[:n_mix], outs[n_mix].reshape(B, D)


def _bdot(a, b):
    return lax.dot_general(a.astype(BF16), b.astype(BF16), (((2,), (1,)), ((0,), (0,))),
                           preferred_element_type=F32)


def _bdot_nt(a, b):
    return lax.dot_general(a.astype(BF16), b.astype(BF16), (((2,), (2,)), ((0,), (0,))),
                           preferred_element_type=F32)


def _bdot_tn(a, b):
    return lax.dot_general(a.astype(BF16), b.astype(BF16), (((1,), (1,)), ((0,), (0,))),
                           preferred_element_type=F32)


def _dot(a, b):
    return jnp.dot(a.astype(BF16), b.astype(BF16), preferred_element_type=F32)


def _dot_nt(a, b):
    return lax.dot_general(a.astype(BF16), b.astype(BF16), (((1,), (1,)), ((), ())),
                           preferred_element_type=F32)


def _head_sum(x, m0):
    s0 = jnp.sum(jnp.where(m0, x, 0.0), axis=-1, keepdims=True)
    s1 = jnp.sum(jnp.where(m0, 0.0, x), axis=-1, keepdims=True)
    return jnp.where(m0, s0, s1)


def _split_heads(x, m0):
    x0 = jnp.where(m0, x, 0.0)
    return jnp.concatenate([x0, x - x0], axis=-2)


def _rwkv_prep(r, k, v, wl, al, g, w0, a0, k_k, k_a, r_k):
    C = RW_CHUNK
    H = RW_HEAD
    P = r.shape[0]
    lane = lax.broadcasted_iota(jnp.int32, (1, 1, LANES), 2)
    m0 = lane < H

    log_decay = -DECAY_SCALE * _sigmoid(w0 + wl)
    a = _sigmoid(a0 + al)
    kk = k * k_k
    kk = kk * lax.rsqrt(jnp.maximum(_head_sum(kk * kk, m0), 1e-24))
    k2 = k * (1.0 + (a - 1.0) * k_a)
    b = kk * a

    ti = lax.broadcasted_iota(jnp.int32, (P, C, C), 1)
    si = lax.broadcasted_iota(jnp.int32, (P, C, C), 2)
    tri = jnp.where(ti >= si, 1.0, 0.0).astype(BF16)
    p_hi = log_decay.astype(BF16)
    rem = log_decay - p_hi.astype(F32)
    p_mid = rem.astype(BF16)
    p_lo = (rem - p_mid.astype(F32)).astype(BF16)
    cum3 = _bdot(tri, jnp.concatenate([p_hi, p_mid, p_lo], axis=2))
    cum = cum3[:, :, :LANES] + cum3[:, :, LANES:2 * LANES] + cum3[:, :, 2 * LANES:]

    mid = cum[:, C // 2 - 1:C // 2, :]
    last = cum[:, C - 1:C, :]
    e_fwd = jnp.exp(cum - mid)
    e_inv = jnp.exp(mid - cum)
    e_prev = jnp.exp(cum - log_decay - mid)
    d_mid = jnp.exp(mid)
    d_end_mid = jnp.exp(last - mid)
    d_end = jnp.exp(last)

    kk_t = kk * e_prev
    r_t = r * e_fwd
    b_t = b * e_inv
    k_t = k2 * e_inv
    bonus = _head_sum(r * k2 * r_k, m0) * v
    return kk_t, r_t, b_t, k_t, v, d_mid, d_end_mid, d_end, bonus, g


def _rwkv_chain(prep, lnx_w, lnx_b, S):
    kk_t, r_t, b_t, k_t, v, d_mid, d_end_mid, d_end, bonus, g = prep
    C = RW_CHUNK
    H = RW_HEAD
    lane = lax.broadcasted_iota(jnp.int32, (1, 1, LANES), 2)
    m0 = lane < H
    q = jnp.concatenate([kk_t, r_t], axis=1)
    q0 = jnp.where(m0, q, 0.0)
    g0 = _bdot_nt(q0, jnp.concatenate([b_t, k_t], axis=1))
    g1 = _bdot_nt(q - q0, jnp.concatenate([k_t, b_t], axis=1))

    t_row = lax.broadcasted_iota(jnp.int32, (1, C, LANES), 1)
    s_col = lax.broadcasted_iota(jnp.int32, (1, C, LANES), 2) & (C - 1)
    strict = t_row > s_col
    incl = t_row >= s_col
    g0t, g0b, g1t, g1b = g0[:, :C], g0[:, C:], g1[:, :C], g1[:, C:]
    l_cat = jnp.where(strict, jnp.where(m0, g0t, g1t), 0.0)
    ak_cat = jnp.where(strict, jnp.where(m0, g1t, g0t), 0.0)
    gb0 = jnp.where(incl, g0b, 0.0)
    gb1 = jnp.where(incl, g1b, 0.0)
    grb_cat = jnp.where(m0, gb0, gb1)

    rr = lax.broadcasted_iota(jnp.int32, (1, LANES, LANES), 1)
    cc = lax.broadcasted_iota(jnp.int32, (1, LANES, LANES), 2)
    bd_mask = (rr < H) == (cc < H)

    def block_diag(x_cat):
        return jnp.where(bd_mask, jnp.concatenate([x_cat, x_cat], axis=1), 0.0)

    t_cat = jnp.where(t_row == s_col, 1.0, 0.0) - l_cat
    m_cat = l_cat
    n = 1
    while 2 * n < C:
        m_cat = _bdot(m_cat, block_diag(m_cat))
        t_cat = t_cat + _bdot(t_cat, block_diag(m_cat))
        n *= 2

    v0 = jnp.where(m0, v, 0.0)
    v1 = v - v0
    kk_h = _bdot(t_cat, _split_heads(kk_t, m0))
    av = _bdot(ak_cat, jnp.concatenate([v1, v0], axis=1))
    u0 = -_bdot(t_cat, _split_heads(av, m0))
    r_h = r_t - _bdot(grb_cat, _split_heads(kk_h, m0))
    u00 = jnp.where(m0, u0, 0.0)
    y0 = _bdot(jnp.concatenate([gb0, gb1], axis=2),
               jnp.concatenate([u00, v0, v1, u0 - u00], axis=1))
    b_end = b_t * d_end_mid
    k_end = k_t * d_end_mid
    s_add = jnp.where(bd_mask, _bdot_tn(jnp.concatenate([u0, v], axis=1),
                                        jnp.concatenate([b_end, k_end], axis=1)), 0.0)
    kb = jnp.where(bd_mask, _bdot_tn(kk_h, b_end), 0.0)

    s_mid = S * d_mid
    y = _bdot_nt(r_h, s_mid) + y0
    s_new = S * d_end - _bdot(s_mid, kb) + s_add

    inv_h = 1.0 / H
    mu = _head_sum(y, m0) * inv_h
    yc = y - mu
    var = _head_sum(yc * yc, m0) * inv_h
    yn = yc * lax.rsqrt(var + GN_EPS) * lnx_w + lnx_b
    return (yn + bonus) * g, s_new


def _rwkv_chunk_body(r_ref, k_ref, v_ref, wl_ref, al_ref, g_ref,
                     w0_ref, a0_ref, kk_ref, ka_ref, rk_ref, lw_ref, lb_ref,
                     y_ref, s_out_ref, s_ref, *, n_pairs, n_sub, n_chunks):
    c = pl.program_id(2)

    @pl.when(c == 0)
    def _():
        s_ref[...] = jnp.zeros_like(s_ref)

    C = RW_CHUNK

    def pairs(ref, lead):
        return jnp.stack([ref[lead + (slice(p * LANES, (p + 1) * LANES),)]
                          for p in range(n_pairs)], axis=0)

    par = [pairs(ref, (slice(None),))
           for ref in (w0_ref, a0_ref, kk_ref, ka_ref, rk_ref, lw_ref, lb_ref)]

    def prep(j):
        rows = (0, slice(j * C, (j + 1) * C))
        tok = [pairs(ref, rows) for ref in (r_ref, k_ref, v_ref, wl_ref, al_ref, g_ref)]
        return _rwkv_prep(*tok, *par[:5])

    S = s_ref[...]
    nxt = prep(0)
    for j in range(n_sub):
        cur = nxt
        if j + 1 < n_sub:
            nxt = prep(j + 1)
        yg, S = _rwkv_chain(cur, par[5], par[6], S)
        for p in range(n_pairs):
            y_ref[0, j * C:(j + 1) * C, p * LANES:(p + 1) * LANES] = yg[p].astype(y_ref.dtype)
    s_ref[...] = S

    @pl.when(c == n_chunks - 1)
    def _():
        s_out_ref[0] = S


def _rwkv_prompt(r, k, v, wl, al, g, w0, a0, k_k, k_a, r_k, lnx_w, lnx_b):
    B, T, D = r.shape
    LW = RW_LANES_PER_STEP
    rows = RW_CHUNK * RW_SUBCHUNKS
    n_pairs = LW // LANES
    n_chunks = T // rows
    tok = pl.BlockSpec((1, rows, LW), lambda b, h, c: (b, c, h))
    par = pl.BlockSpec((1, LW), lambda b, h, c: (0, h))
    body = functools.partial(_rwkv_chunk_body, n_pairs=n_pairs, n_sub=RW_SUBCHUNKS,
                             n_chunks=n_chunks)
    y, s_bd = pl.pallas_call(
        body,
        grid=(B, D // LW, n_chunks),
        in_specs=[tok] * 6 + [par] * 7,
        out_specs=[pl.BlockSpec((1, rows, LW), lambda b, h, c: (b, c, h)),
                   pl.BlockSpec((1, n_pairs, LANES, LANES), lambda b, h, c: (b, h, 0, 0))],
        out_shape=[jax.ShapeDtypeStruct((B, T, D), BF16),
                   jax.ShapeDtypeStruct((B, D // LANES, LANES, LANES), F32)],
        scratch_shapes=[pltpu.VMEM((n_pairs, LANES, LANES), F32)],
        compiler_params=_cparams(("parallel", "parallel", "arbitrary")),
        name="rwkv_chunk",
    )(r, k, v, wl, al, g, *[p.reshape(1, D) for p in (w0, a0, k_k, k_a, r_k, lnx_w, lnx_b)])
    H = RW_HEAD
    s = jnp.stack([s_bd[:, :, :H, :H], s_bd[:, :, H:, H:]], axis=2)
    return y, s.reshape(B, D // H, H, H)


def _rwkv_step_body(s_ref, d_ref, kk_ref, b_ref, k_ref, r_ref, v_ref, s_out_ref, y_ref):
    S = s_ref[0]
    s_kk = jnp.sum(S * kk_ref[0], axis=-1, keepdims=True)
    S = S * d_ref[0] - s_kk * b_ref[0] + v_ref[0] * k_ref[0]
    s_out_ref[0] = S
    y_ref[0] = jnp.sum(S * r_ref[0], axis=-1, keepdims=True)


def _rwkv_step(S0, decay, kk, b, k, r, v):
    B, Hh, N, _ = S0.shape
    row = lambda t: t.reshape(B, Hh, 1, N)
    col = lambda t: t.reshape(B, Hh, N, 1)
    row_spec = pl.BlockSpec((1, Hh, 1, N), lambda i: (i, 0, 0, 0))
    col_spec = pl.BlockSpec((1, Hh, N, 1), lambda i: (i, 0, 0, 0))
    st_spec = pl.BlockSpec((1, Hh, N, N), lambda i: (i, 0, 0, 0))
    S1, y = pl.pallas_call(
        _rwkv_step_body,
        grid=(B,),
        in_specs=[st_spec] + [row_spec] * 5 + [col_spec],
        out_specs=[st_spec, col_spec],
        out_shape=[jax.ShapeDtypeStruct(S0.shape, F32), jax.ShapeDtypeStruct((B, Hh, N, 1), F32)],
        compiler_params=_cparams(("parallel",)),
        name="rwkv_step",
    )(S0, row(decay), row(kk), row(b), row(k), row(r), col(v))
    return S1, y.reshape(B, Hh * N)


def _swa_prompt_body(q_ref, kp_ref, kc_ref, vp_ref, vc_ref, qg_ref, sink_ref, o_ref):
    i = pl.program_id(1)
    W = WINDOW
    lane = lax.broadcasted_iota(jnp.int32, (1, LANES), 1)
    m0 = lane < HEAD_DIM
    a_idx = lax.broadcasted_iota(jnp.int32, (W, 2 * W), 0)
    c_idx = lax.broadcasted_iota(jnp.int32, (W, 2 * W), 1)
    first_key = jnp.where(i > 0, 0, W)
    mask = (c_idx > a_idx) & (c_idx <= a_idx + W) & (c_idx >= first_key)
    neg = jnp.float32(-jnp.inf)
    qg = qg_ref[...]

    for blk in range(N_KV_HEADS // 2):
        ksl = slice(blk * LANES, (blk + 1) * LANES)
        kb = jnp.concatenate([kp_ref[0, :, ksl], kc_ref[0, :, ksl]], axis=0)
        vb = jnp.concatenate([vp_ref[0, :, ksl], vc_ref[0, :, ksl]], axis=0)
        kb_r = pltpu.roll(kb, HEAD_DIM, 1)
        vb_r = pltpu.roll(vb, HEAD_DIM, 1)
        for sub in range(2):
            if sub == 0:
                k2 = jnp.where(m0, kb, kb_r)
                v2 = jnp.where(m0, vb, vb_r)
            else:
                k2 = jnp.where(m0, kb_r, kb)
                v2 = jnp.where(m0, vb_r, vb)
            k2 = k2.astype(BF16)
            v2 = v2.astype(BF16)
            kv = 2 * blk + sub
            for pp in range(GQA_GROUP // 2):
                pair = kv * (GQA_GROUP // 2) + pp
                qsl = slice(pair * LANES, (pair + 1) * LANES)
                qv = q_ref[0, :, qsl]
                ms = _head_sum(qv * qv, m0) * (1.0 / HEAD_DIM)
                qn = qv * lax.rsqrt(ms + RMS_EPS) * qg * ATTN_SCALE
                sv = sink_ref[:, qsl]
                outs = []
                for h in range(2):
                    hm = m0 if h == 0 else jnp.logical_not(m0)
                    s = lax.dot_general(jnp.where(hm, qn, 0.0).astype(BF16), k2,
                                        (((1,), (1,)), ((), ())), preferred_element_type=F32)
                    s = jnp.where(mask, s, neg)
                    sink = jnp.max(jnp.where(hm, sv, neg), axis=-1, keepdims=True)
                    m = jnp.maximum(jnp.max(s, axis=-1, keepdims=True), sink)
                    p = jnp.exp(s - m)
                    denom = jnp.sum(p, axis=-1, keepdims=True) + jnp.exp(sink - m)
                    p = p / denom
                    outs.append(jnp.dot(p.astype(BF16), v2, preferred_element_type=F32))
                o_ref[0, :, qsl] = jnp.where(m0, outs[0], outs[1]).astype(o_ref.dtype)


def _swa_prompt(q, k, v, q_norm_g, sinks):
    B, T, D = q.shape
    KVW = k.shape[-1]
    W = WINDOW
    qg = jnp.tile(q_norm_g.astype(F32), 2).reshape(1, LANES)
    sink_l = jnp.repeat(sinks.astype(F32), HEAD_DIM).reshape(1, D)
    cur = pl.BlockSpec((1, W, KVW), lambda b, i: (b, i, 0))
    prev = pl.BlockSpec((1, W, KVW), lambda b, i: (b, jnp.maximum(i - 1, 0), 0))
    return pl.pallas_call(
        _swa_prompt_body,
        grid=(B, T // W),
        in_specs=[pl.BlockSpec((1, W, D), lambda b, i: (b, i, 0)), prev, cur, prev, cur,
                  pl.BlockSpec((1, LANES), lambda b, i: (0, 0)),
                  pl.BlockSpec((1, D), lambda b, i: (0, 0))],
        out_specs=pl.BlockSpec((1, W, D), lambda b, i: (b, i, 0)),
        out_shape=jax.ShapeDtypeStruct((B, T, D), BF16),
        compiler_params=_cparams(("parallel", "parallel")),
        name="swa_prompt",
    )(q, k, k, v, v, qg, sink_l)


def _swa_sample_body(q_ref, kc_ref, vc_ref, kn_ref, vn_ref, qg_ref, sink_ref, o_ref):
    G, HD = GQA_GROUP, HEAD_DIM
    q = q_ref[0]
    ms = jnp.mean(q * q, axis=-1, keepdims=True)
    qn = q * lax.rsqrt(ms + RMS_EPS) * qg_ref[...] * ATTN_SCALE
    n_buf = kc_ref.shape[1]
    col = lax.broadcasted_iota(jnp.int32, (G, n_buf), 1)
    valid = (n_buf - col) < WINDOW
    neg = jnp.float32(-jnp.inf)
    for kv in range(N_KV_HEADS):
        qj = qn[kv * G:(kv + 1) * G, :]
        lsl = slice(kv * HD, (kv + 1) * HD)
        kc = kc_ref[0, :, lsl]
        vc = vc_ref[0, :, lsl]
        kn = kn_ref[0, :, lsl]
        vn = vn_ref[0, :, lsl]
        s_c = jnp.where(valid, _dot_nt(qj, kc), neg)
        s_n = jnp.sum(qj.astype(BF16).astype(F32) * kn.astype(BF16).astype(F32),
                      axis=-1, keepdims=True)
        sink = sink_ref[kv * G:(kv + 1) * G, :]
        m = jnp.maximum(jnp.maximum(jnp.max(s_c, axis=-1, keepdims=True), s_n), sink)
        p_c = jnp.exp(s_c - m)
        p_n = jnp.exp(s_n - m)
        denom = jnp.sum(p_c, axis=-1, keepdims=True) + p_n + jnp.exp(sink - m)
        p_c = p_c / denom
        p_n = p_n / denom
        o = _dot(p_c, vc) + p_n.astype(BF16).astype(F32) * vn.astype(BF16).astype(F32)
        o_ref[0, kv * G:(kv + 1) * G, :] = o.astype(o_ref.dtype)


def _swa_sample(q, k_buf, v_buf, k_new, v_new, q_norm_g, sinks):
    B, D = q.shape
    n_buf, KVW = k_buf.shape[1:]
    NH, HD = N_Q_HEADS, HEAD_DIM
    buf = pl.BlockSpec((1, n_buf, KVW), lambda b: (b, 0, 0))
    new = pl.BlockSpec((1, 1, KVW), lambda b: (b, 0, 0))
    o = pl.pallas_call(
        _swa_sample_body,
        grid=(B,),
        in_specs=[pl.BlockSpec((1, NH, HD), lambda b: (b, 0, 0)), buf, buf, new, new,
                  pl.BlockSpec((1, HD), lambda b: (0, 0)),
                  pl.BlockSpec((NH, 1), lambda b: (0, 0))],
        out_specs=pl.BlockSpec((1, NH, HD), lambda b: (b, 0, 0)),
        out_shape=jax.ShapeDtypeStruct((B, NH, HD), BF16),
        compiler_params=_cparams(("parallel",)),
        name="swa_sample",
    )(q.reshape(B, NH, HD), k_buf, v_buf, k_new, v_new,
      q_norm_g.astype(F32).reshape(1, HD), sinks.astype(F32).reshape(NH, 1))
    return o.reshape(B, D)


def _rms(x, g):
    return x * lax.rsqrt(jnp.mean(x * x, axis=-1, keepdims=True) + RMS_EPS) * g


def _silu(x):
    return x * jax.nn.sigmoid(x)


def _forward(x, mods, kvmod, wkv0, shift0, conv0, k_buf, v_buf, p, wsrc, tag):
    B, T, D = x.shape
    M = B * T
    depth = len(mods)
    n_a = depth // 2
    emit = T == 1
    wb = {}

    def mm(xin, wname, l=None, **kw):
        if emit:
            out, w16 = _mm(xin, wsrc[wname], layer=l, emit_w=True, name=f"{tag}_{wname}", **kw)
            wb[(wname, l)] = w16
            return out
        w = wsrc[wname] if l is None else wsrc[wname][l]
        return _mm(xin, w, name=f"{tag}_{wname}", **kw)

    new_wkv, new_shift, new_conv = [], [], []
    k_att = v_att = k_win = v_win = None
    x2 = x.reshape(M, D)
    for l in range(depth):
        sh1, sc1, gt1, sh2, sc2, gt2 = jnp.split(mods[l], 6, axis=-1)
        x3 = x2.reshape(B, T, D)
        if l < n_a:
            mixes, h_last = _norm_shift(x3, p['ln1_g'][l], sc1, sh1, shift0[l], p['rwkv_mix'][l])
            xr, xw, xk, xv, xa, xg = [m.reshape(M, D) for m in mixes]
            r = mm(xr, 'rwkv_w_r', l)
            k = mm(xk, 'rwkv_w_k', l)
            v = mm(xv, 'rwkv_w_v', l)
            wl = mm(mm(xw, 'rwkv_w1', l, act="tanh", out_dtype=BF16), 'rwkv_w2', l)
            al = mm(mm(xa, 'rwkv_a1', l, out_dtype=BF16), 'rwkv_a2', l)
            g = mm(mm(xg, 'rwkv_g1', l, act="sigmoid", out_dtype=BF16), 'rwkv_g2', l)
            w0, a0 = p['rwkv_w0'][l], p['rwkv_a0'][l]
            k_k, k_a = p['rwkv_k_k'][l], p['rwkv_k_a'][l]
            r_k = p['rwkv_r_k'][l].reshape(D)
            lnx_w, lnx_b = p['rwkv_lnx_w'][l], p['rwkv_lnx_b'][l]
            if T > 1:
                sh = (B, T, D)
                yg, S = _rwkv_prompt(r.reshape(sh), k.reshape(sh), v.reshape(sh), wl.reshape(sh),
                                     al.reshape(sh), g.reshape(sh), w0, a0, k_k, k_a, r_k,
                                     lnx_w, lnx_b)
                yg = yg.reshape(M, D)
            else:
                Hh, N = RW_HEADS, RW_HEAD
                w_log = -jax.nn.softplus(-(w0 + wl)) - 0.5
                decay = jnp.exp(-jnp.exp(w_log))
                a = jax.nn.sigmoid(a0 + al)
                kk = (k * k_k).reshape(B, Hh, N)
                kk = kk / jnp.maximum(jnp.sqrt(jnp.sum(kk * kk, axis=-1, keepdims=True)), 1e-12)
                kk = kk.reshape(B, D)
                k2 = k * (1.0 + (a - 1.0) * k_a)
                S, y = _rwkv_step(wkv0[l].astype(F32), decay, kk, kk * a, k2, r, v)
                y4 = y.reshape(B, Hh, N)
                mu = jnp.mean(y4, axis=-1, keepdims=True)
                var = jnp.mean(jnp.square(y4 - mu), axis=-1, keepdims=True)
                y4 = ((y4 - mu) * lax.rsqrt(var + GN_EPS) * lnx_w.reshape(Hh, N)
                      + lnx_b.reshape(Hh, N))
                bonus = jnp.sum((r * k2 * r_k).reshape(B, Hh, N), axis=-1, keepdims=True)
                y4 = y4 + bonus * v.reshape(B, Hh, N)
                yg = (y4.reshape(B, D) * g).astype(BF16)
            new_wkv.append(S)
            new_shift.append(h_last)
            x2 = mm(yg, 'rwkv_w_o', l, res=x2, gate=gt1, rows_per_gate=T)
        else:
            j = l - n_a
            h = _norm_mod(x3, p['ln1_g'][l], sc1, sh1).reshape(M, D)
            q = mm(h, 'attn_w_q', j)
            if k_buf is None:
                attn = _swa_prompt(q.reshape(B, T, D), k_att, v_att, p['attn_q_norm_g'][j],
                                   p['attn_sinks'][j]).reshape(M, D)
            else:
                attn = _swa_sample(q, k_buf, v_buf, k_att, v_att, p['attn_q_norm_g'][j],
                                   p['attn_sinks'][j])
            x2 = mm(attn, 'attn_w_o', j, res=x2, gate=gt1, rows_per_gate=T)
        h2 = _norm_mod(x2.reshape(B, T, D), p['ln2_g'][l], sc2, sh2).reshape(M, D)
        if emit:
            act, cb, wg16, wu16 = _ffn_in_tok(h2, wsrc['ffn_w_in'], l, p['ffn_conv_w'][l],
                                              p['ffn_conv_b'][l], conv0[l])
            wb[('ffn_w_gate', l)] = wg16
            wb[('ffn_w_up', l)] = wu16
        else:
            act, cb = _ffn_in_seq(h2, wsrc['ffn_w_gate'][l], wsrc['ffn_w_up'][l],
                                  p['ffn_conv_w'][l], p['ffn_conv_b'][l], conv0[l], T)
        new_conv.append(cb)
        x2 = mm(act, 'ffn_w_out', l, res=x2, gate=gt2, rows_per_gate=T)
        if l == n_a - 1:
            sh, sc = jnp.split(kvmod, 2, axis=-1)
            hn = _norm_mod(x2.reshape(B, T, D), p['kv_norm_g'], sc, sh).reshape(M, D)
            kv = mm(hn, 'w_kv')
            KVW = kv.shape[-1] // 2
            k_new = _rms(kv[:, :KVW].reshape(B, T, N_KV_HEADS, HEAD_DIM), p['k_norm_g'])
            v_new = kv[:, KVW:].reshape(B, T, N_KV_HEADS, HEAD_DIM)
            k_att, v_att = k_new.reshape(B, T, KVW), v_new.reshape(B, T, KVW)
            if k_buf is None:
                w = min(WINDOW, T)
                k_win, v_win = k_new[:, T - w:], v_new[:, T - w:]
            else:
                n_buf = k_buf.shape[1]
                k_win = jnp.concatenate([k_buf, k_new], axis=1)[:, -n_buf:]
                v_win = jnp.concatenate([v_buf, v_new], axis=1)[:, -n_buf:]
                k_buf = k_buf.reshape(B, n_buf, KVW)
                v_buf = v_buf.reshape(B, n_buf, KVW)
    outs = (x2.reshape(B, T, D), jnp.stack(new_wkv), jnp.stack(new_shift), jnp.stack(new_conv),
            k_win, v_win)
    return outs, wb


class _Layered:
    def __init__(self, wb):
        self._wb = wb

    def __getitem__(self, name):
        if (name, None) in self._wb:
            return self._wb[(name, None)]
        n = 1 + max(l for (nm, l) in self._wb if nm == name)
        return [self._wb[(name, l)] for l in range(n)]


def kernel(x_prompt, x_sample, c_prompt, c_sample, state_wkv, state_shift, state_conv, cache_k_win, cache_v_win, mod_w, mod_b, ln1_g, ln2_g, rwkv_mix, rwkv_w0, rwkv_w1, rwkv_w2, rwkv_a0, rwkv_a1, rwkv_a2, rwkv_g1, rwkv_g2, rwkv_k_k, rwkv_k_a, rwkv_r_k, rwkv_w_r, rwkv_w_k, rwkv_w_v, rwkv_w_o, rwkv_lnx_w, rwkv_lnx_b, kv_norm_g, kv_mod_w, kv_mod_b, w_kv, k_norm_g, attn_w_q, attn_q_norm_g, attn_sinks, attn_w_o, ffn_w_in, ffn_conv_w, ffn_conv_b, ffn_w_out):
    p = dict(mod_w=mod_w, mod_b=mod_b, ln1_g=ln1_g, ln2_g=ln2_g,
             rwkv_mix=rwkv_mix, rwkv_w0=rwkv_w0, rwkv_w1=rwkv_w1, rwkv_w2=rwkv_w2,
             rwkv_a0=rwkv_a0, rwkv_a1=rwkv_a1, rwkv_a2=rwkv_a2, rwkv_g1=rwkv_g1, rwkv_g2=rwkv_g2,
             rwkv_k_k=rwkv_k_k, rwkv_k_a=rwkv_k_a, rwkv_r_k=rwkv_r_k, rwkv_w_r=rwkv_w_r,
             rwkv_w_k=rwkv_w_k, rwkv_w_v=rwkv_w_v, rwkv_w_o=rwkv_w_o,
             rwkv_lnx_w=rwkv_lnx_w, rwkv_lnx_b=rwkv_lnx_b,
             kv_norm_g=kv_norm_g, kv_mod_w=kv_mod_w, kv_mod_b=kv_mod_b, w_kv=w_kv, k_norm_g=k_norm_g,
             attn_w_q=attn_w_q, attn_q_norm_g=attn_q_norm_g, attn_sinks=attn_sinks, attn_w_o=attn_w_o,
             ffn_w_in=ffn_w_in, ffn_conv_w=ffn_conv_w, ffn_conv_b=ffn_conv_b, ffn_w_out=ffn_w_out)
    depth = mod_w.shape[0]
    n_a = depth // 2
    Bp = x_prompt.shape[0]
    dt = x_prompt.dtype

    c_all = _silu(jnp.concatenate([c_prompt, c_sample], axis=0)).astype(BF16)
    mods = [_mm(c_all, mod_w, layer=l, bias=mod_b[l], name=f"mod{l}") for l in range(depth)]
    kvmod = _mm(c_all, kv_mod_w, bias=kv_mod_b, name="kvmod")

    outs_s, wb = _forward(x_sample, [m[Bp:] for m in mods], kvmod[Bp:], state_wkv, state_shift,
                          state_conv, cache_k_win, cache_v_win, p, p, "s")
    F = ffn_conv_b.shape[-1]
    wkv0 = jnp.zeros((n_a, Bp, RW_HEADS, RW_HEAD, RW_HEAD), dt)
    shift0 = jnp.zeros((n_a, Bp, D_MODEL), dt)
    conv0 = jnp.zeros((depth, Bp, CONV_W - 1, F), dt)
    outs_p, _ = _forward(x_prompt, [m[:Bp] for m in mods], kvmod[:Bp], wkv0, shift0, conv0,
                         None, None, p, _Layered(wb), "p")
    y_p, wkv_p, shift_p, conv_p, kwin_p, vwin_p = outs_p
    y_s, wkv_s, shift_s, conv_s, kwin_s, vwin_s = outs_s
    return (y_p, y_s, wkv_p, wkv_s, shift_p, shift_s, conv_p, conv_s,
            kwin_p, kwin_s, vwin_p, vwin_s)
```

```python
import functools

import jax
import jax.numpy as jnp
from jax import lax
from jax.experimental import pallas as pl
from jax.experimental.pallas import tpu as pltpu

F32 = jnp.float32
BF16 = jnp.bfloat16

D_MODEL = 4096
RW_HEAD = 64
RW_HEADS = D_MODEL // RW_HEAD
GN_EPS = 64e-5
DECAY_SCALE = 0.6065306597126334
RMS_EPS = 1e-6
HEAD_DIM = 64
N_Q_HEADS = D_MODEL // HEAD_DIM
N_KV_HEADS = N_Q_HEADS // 8
GQA_GROUP = N_Q_HEADS // N_KV_HEADS
WINDOW = 128
ATTN_SCALE = HEAD_DIM ** -0.5
CONV_W = 3

LANES = 128
SUBLANES = 8
MXU_COLS = 256
VMEM_LIMIT_BYTES = 56 * 1024 * 1024

RW_CHUNK = 64
RW_LANES_PER_STEP = 2048
RW_SUBCHUNKS = 2
NORM_ROWS = 256


def _cparams(sem):
    return pltpu.CompilerParams(dimension_semantics=sem, vmem_limit_bytes=VMEM_LIMIT_BYTES)


def _pick_tile(dim, prefs):
    for p in prefs:
        if dim >= p and dim % p == 0:
            return p
    return dim


def _sigmoid(x):
    return 1.0 / (1.0 + jnp.exp(-x))


def _mm_body(*refs, nk, has_bias, act, has_res, emit_w):
    x_ref, w_ref = refs[0], refs[1]
    pos = 2
    bias_ref = res_ref = gate_ref = wout_ref = None
    if has_bias:
        bias_ref = refs[pos]; pos += 1
    if has_res:
        res_ref, gate_ref = refs[pos], refs[pos + 1]; pos += 2
    o_ref = refs[pos]; pos += 1
    if emit_w:
        wout_ref = refs[pos]; pos += 1
    acc_ref = refs[pos] if nk > 1 else None

    w = w_ref[...].astype(BF16)
    if emit_w:
        wout_ref[...] = w
    part = jnp.dot(x_ref[...].astype(BF16), w, preferred_element_type=F32)

    def finish(acc):
        if has_bias:
            acc = acc + bias_ref[...]
        if act == "tanh":
            acc = jnp.tanh(acc)
        elif act == "sigmoid":
            acc = _sigmoid(acc)
        if has_res:
            acc = res_ref[...] + gate_ref[...] * acc
        o_ref[...] = acc.astype(o_ref.dtype)

    if nk == 1:
        finish(part)
    else:
        k = pl.program_id(2)

        @pl.when(k == 0)
        def _():
            acc_ref[...] = part

        @pl.when(k > 0)
        def _():
            acc_ref[...] += part

        @pl.when(k == nk - 1)
        def _():
            finish(acc_ref[...])


def _mm(x, w, *, layer=None, bias=None, act=None, res=None, gate=None, rows_per_gate=1,
        out_dtype=F32, emit_w=False, name="mm"):
    M, K = x.shape
    K2, N = w.shape[-2:]
    assert K == K2 and (w.ndim == 2) == (layer is None)
    tm = _pick_tile(M, (1024,))
    tn = _pick_tile(N, (1024, 512))
    tk = _pick_tile(K, (2048,)) if M < 1024 or K <= 2048 else (K if K <= 4096 else
                                                                _pick_tile(K, (3584, 2048)))
    nk = K // tk
    grid = (M // tm, N // tn, nk)
    assert not emit_w or grid[0] == 1

    if layer is None:
        w_spec = pl.BlockSpec((tk, tn), lambda i, j, k: (k, j))
    else:
        w_spec = pl.BlockSpec((None, tk, tn), lambda i, j, k: (layer, k, j))
    in_specs = [pl.BlockSpec((tm, tk), lambda i, j, k: (i, k)), w_spec]
    args = [x, w]
    if bias is not None:
        in_specs.append(pl.BlockSpec((1, tn), lambda i, j, k: (0, j)))
        args.append(bias.reshape(1, N).astype(F32))
    if res is not None:
        in_specs.append(pl.BlockSpec((tm, tn), lambda i, j, k: (i, j)))
        args.append(res)
        if rows_per_gate == 1:
            in_specs.append(pl.BlockSpec((tm, tn), lambda i, j, k: (i, j)))
            args.append(gate)
        else:
            assert rows_per_gate % tm == 0
            bpg = rows_per_gate // tm
            in_specs.append(pl.BlockSpec((None, 1, tn), lambda i, j, k: (i // bpg, 0, j)))
            args.append(gate.reshape(gate.shape[0], 1, N))
    out_specs = [pl.BlockSpec((tm, tn), lambda i, j, k: (i, j))]
    out_shape = [jax.ShapeDtypeStruct((M, N), out_dtype)]
    if emit_w:
        out_specs.append(pl.BlockSpec((tk, tn), lambda i, j, k: (k, j)))
        out_shape.append(jax.ShapeDtypeStruct((K, N), BF16))
    scratch = [pltpu.VMEM((tm, tn), F32)] if nk > 1 else []
    body = functools.partial(_mm_body, nk=nk, has_bias=bias is not None, act=act,
                             has_res=res is not None, emit_w=emit_w)
    outs = pl.pallas_call(
        body,
        grid=grid,
        in_specs=in_specs,
        out_specs=out_specs,
        out_shape=out_shape,
        scratch_shapes=scratch,
        compiler_params=_cparams(("parallel", "parallel", "arbitrary")),
        name=name,
    )(*args)
    return tuple(outs) if emit_w else outs[0]


def _accumulate(k, nk, pairs):
    if nk == 1:
        return

    @pl.when(k == 0)
    def _():
        for acc_ref, part in pairs:
            acc_ref[...] = part

    @pl.when(k > 0)
    def _():
        for acc_ref, part in pairs:
            acc_ref[...] += part


def _ffn_in_seq_body(x_ref, wg_ref, wu_ref, cw_ref, cb_ref, st_ref,
                     act_ref, st_out_ref, carry, *, tiles_per_seq, n_sub):
    i = pl.program_id(1)

    @pl.when(i % tiles_per_seq == 0)
    def _():
        carry[...] = st_ref[...]

    x = x_ref[...]
    tm = x.shape[0]
    ts = act_ref.shape[1] // n_sub
    row = lax.broadcasted_iota(jnp.int32, (SUBLANES, ts), 0)
    for s in range(n_sub):
        sl = slice(s * ts, (s + 1) * ts)
        g = jnp.dot(x, wg_ref[:, sl], preferred_element_type=F32)
        u = jnp.dot(x, wu_ref[:, sl], preferred_element_type=F32)
        tail = carry[:, sl]
        before2, before1 = tail[0:1], tail[1:2]
        r1 = pltpu.roll(g, 1, 0)
        r2 = pltpu.roll(g, 2, 0)
        head1 = jnp.where(row == 0, before1, r1[:SUBLANES])
        head2 = jnp.where(row == 0, before2, jnp.where(row == 1, before1, r2[:SUBLANES]))
        prev1 = jnp.concatenate([head1, r1[SUBLANES:]], axis=0)
        prev2 = jnp.concatenate([head2, r2[SUBLANES:]], axis=0)
        cw = cw_ref[:, sl]
        conv = cb_ref[:, sl] + cw[0:1] * prev2 + cw[1:2] * prev1 + cw[2:3] * g
        act_ref[:, sl] = (conv * _sigmoid(conv) * u).astype(act_ref.dtype)
        last = g[tm - (CONV_W - 1):, :]
        st_out_ref[:, sl] = last
        carry[:, sl] = last


def _ffn_in_seq(x, wg, wu, conv_w, conv_b, state, T):
    M, K = x.shape
    F = wg.shape[1]
    tm = _pick_tile(T, (512,))
    tn = _pick_tile(F, (1024, 512))
    tps = T // tm
    body = functools.partial(_ffn_in_seq_body, tiles_per_seq=tps, n_sub=tn // MXU_COLS)
    w_spec = pl.BlockSpec((K, tn), lambda j, i: (0, j))
    act, st = pl.pallas_call(
        body,
        grid=(F // tn, M // tm),
        in_specs=[pl.BlockSpec((tm, K), lambda j, i: (i, 0)),
                  w_spec, w_spec,
                  pl.BlockSpec((CONV_W, tn), lambda j, i: (0, j)),
                  pl.BlockSpec((1, tn), lambda j, i: (0, j)),
                  pl.BlockSpec((None, CONV_W - 1, tn), lambda j, i: (i // tps, 0, j))],
        out_specs=[pl.BlockSpec((tm, tn), lambda j, i: (i, j)),
                   pl.BlockSpec((None, CONV_W - 1, tn), lambda j, i: (i, 0, j))],
        out_shape=[jax.ShapeDtypeStruct((M, F), BF16),
                   jax.ShapeDtypeStruct((M // tm, CONV_W - 1, F), F32)],
        scratch_shapes=[pltpu.VMEM((CONV_W - 1, tn), F32)],
        compiler_params=_cparams(("parallel", "arbitrary")),
        name="ffn_in_seq",
    )(x, wg, wu, conv_w.astype(F32), conv_b.reshape(1, F).astype(F32), state.astype(F32))
    return act, st[tps - 1::tps]


def _ffn_in_tok_body(x_ref, wg_ref, wu_ref, cw_ref, cb_ref, s0_ref, s1_ref,
                     act_ref, g_ref, wgo_ref, wuo_ref, accg, accu, *, nk):
    k = pl.program_id(2)
    x = x_ref[...]
    wg = wg_ref[...].astype(BF16)
    wu = wu_ref[...].astype(BF16)
    wgo_ref[...] = wg
    wuo_ref[...] = wu
    pg = jnp.dot(x, wg, preferred_element_type=F32)
    pu = jnp.dot(x, wu, preferred_element_type=F32)
    _accumulate(k, nk, [(accg, pg), (accu, pu)])

    @pl.when(k == nk - 1)
    def _():
        g = accg[...] if nk > 1 else pg
        u = accu[...] if nk > 1 else pu
        cw = cw_ref[...]
        conv = cb_ref[...] + cw[0:1] * s0_ref[...] + cw[1:2] * s1_ref[...] + cw[2:3] * g
        act_ref[...] = (conv * _sigmoid(conv) * u).astype(act_ref.dtype)
        g_ref[...] = g


def _ffn_in_tok(x, w_in, layer, conv_w, conv_b, state):
    B, K = x.shape
    F = w_in.shape[-1] // 2
    tn = _pick_tile(F, (1024, 512))
    tk = _pick_tile(K, (2048,))
    nk = K // tk
    nj = F // tn
    body = functools.partial(_ffn_in_tok_body, nk=nk)
    row_spec = pl.BlockSpec((B, tn), lambda i, j, k: (0, j))
    wo_spec = pl.BlockSpec((tk, tn), lambda i, j, k: (k, j))
    s0 = state[:, 0, :].astype(F32)
    s1 = state[:, 1, :].astype(F32)
    act, g, wg, wu = pl.pallas_call(
        body,
        grid=(1, nj, nk),
        in_specs=[pl.BlockSpec((B, tk), lambda i, j, k: (0, k)),
                  pl.BlockSpec((None, tk, tn), lambda i, j, k: (layer, k, j)),
                  pl.BlockSpec((None, tk, tn), lambda i, j, k: (layer, k, j + nj)),
                  pl.BlockSpec((CONV_W, tn), lambda i, j, k: (0, j)),
                  pl.BlockSpec((1, tn), lambda i, j, k: (0, j)),
                  row_spec, row_spec],
        out_specs=[row_spec, row_spec, wo_spec, wo_spec],
        out_shape=[jax.ShapeDtypeStruct((B, F), BF16), jax.ShapeDtypeStruct((B, F), F32),
                   jax.ShapeDtypeStruct((K, F), BF16), jax.ShapeDtypeStruct((K, F), BF16)],
        scratch_shapes=[pltpu.VMEM((B, tn), F32), pltpu.VMEM((B, tn), F32)],
        compiler_params=_cparams(("parallel", "parallel", "arbitrary")),
        name="ffn_in_tok",
    )(x, w_in, w_in, conv_w.astype(F32), conv_b.reshape(1, F).astype(F32), s0, s1)
    return act, jnp.stack([s1, g], axis=1), wg, wu


def _norm_mod_rows(x, g, sc, sh):
    y = x * lax.rsqrt(jnp.mean(x * x, axis=-1, keepdims=True) + RMS_EPS) * g
    return y * (1.0 + sc) + sh


def _norm_body(x_ref, g_ref, sc_ref, sh_ref, o_ref):
    o_ref[0] = _norm_mod_rows(x_ref[0], g_ref[...], sc_ref[0], sh_ref[0]).astype(o_ref.dtype)


def _norm_mod(x, g, sc, sh):
    B, T, D = x.shape
    tr = _pick_tile(T, (NORM_ROWS,))
    tok = pl.BlockSpec((1, tr, D), lambda b, t: (b, t, 0))
    per_b = pl.BlockSpec((1, 1, D), lambda b, t: (b, 0, 0))
    return pl.pallas_call(
        _norm_body,
        grid=(B, T // tr),
        in_specs=[tok, pl.BlockSpec((1, D), lambda b, t: (0, 0)), per_b, per_b],
        out_specs=tok,
        out_shape=jax.ShapeDtypeStruct((B, T, D), BF16),
        compiler_params=_cparams(("parallel", "parallel")),
        name="norm_mod",
    )(x, g.reshape(1, D), sc.reshape(B, 1, D), sh.reshape(B, 1, D))


def _norm_shift_body(x_ref, g_ref, sc_ref, sh_ref, s0_ref, mix_ref, *rest, n_mix):
    outs = rest[:n_mix]
    last_ref = rest[n_mix]
    carry = rest[n_mix + 1]
    t = pl.program_id(1)

    @pl.when(t == 0)
    def _():
        carry[...] = s0_ref[0]

    h = _norm_mod_rows(x_ref[0], g_ref[...], sc_ref[0], sh_ref[0])
    tr = h.shape[0]
    if tr == 1:
        h_prev = carry[...]
    else:
        row = lax.broadcasted_iota(jnp.int32, h.shape, 0)
        h_prev = jnp.where(row == 0, carry[...], pltpu.roll(h, 1, 0))
    dx = h_prev - h
    mix = mix_ref[...]
    for i in range(n_mix):
        outs[i][0] = (h + dx * mix[i:i + 1]).astype(outs[i].dtype)
    last = h[tr - 1:tr, :]
    carry[...] = last
    last_ref[0] = last


def _norm_shift(x, g, sc, sh, shift_prev, mix):
    B, T, D = x.shape
    n_mix = mix.shape[0]
    tr = _pick_tile(T, (NORM_ROWS,))
    tok = pl.BlockSpec((1, tr, D), lambda b, t: (b, t, 0))
    per_b = pl.BlockSpec((1, 1, D), lambda b, t: (b, 0, 0))
    body = functools.partial(_norm_shift_body, n_mix=n_mix)
    outs = pl.pallas_call(
        body,
        grid=(B, T // tr),
        in_specs=[tok, pl.BlockSpec((1, D), lambda b, t: (0, 0)), per_b, per_b, per_b,
                  pl.BlockSpec((n_mix, D), lambda b, t: (0, 0))],
        out_specs=[tok] * n_mix + [per_b],
        out_shape=[jax.ShapeDtypeStruct((B, T, D), BF16)] * n_mix
        + [jax.ShapeDtypeStruct((B, 1, D), F32)],
        scratch_shapes=[pltpu.VMEM((1, D), F32)],
        compiler_params=_cparams(("parallel", "arbitrary")),
        name="norm_shift",
    )(x, g.reshape(1, D), sc.reshape(B, 1, D), sh.reshape(B, 1, D),
      shift_prev.astype(F32).reshape(B, 1, D), mix.astype(F32))
    return outs[:n_mix], outs[n_mix].reshape(B, D)


def _bdot(a, b):
    return lax.dot_general(a.astype(BF16), b.astype(BF16), (((2,), (1,)), ((0,), (0,))),
                           preferred_element_type=F32)


def _bdot_nt(a, b):
    return lax.dot_general(a.astype(BF16), b.astype(BF16), (((2,), (2,)), ((0,), (0,))),
                           preferred_element_type=F32)


def _bdot_tn(a, b):
    return lax.dot_general(a.astype(BF16), b.astype(BF16), (((1,), (1,)), ((0,), (0,))),
                           preferred_element_type=F32)


def _dot(a, b):
    return jnp.dot(a.astype(BF16), b.astype(BF16), preferred_element_type=F32)


def _dot_nt(a, b):
    return lax.dot_general(a.astype(BF16), b.astype(BF16), (((1,), (1,)), ((), ())),
                           preferred_element_type=F32)


def _head_sum(x, m0):
    s0 = jnp.sum(jnp.where(m0, x, 0.0), axis=-1, keepdims=True)
    s1 = jnp.sum(jnp.where(m0, 0.0, x), axis=-1, keepdims=True)
    return jnp.where(m0, s0, s1)


def _split_heads(x, m0):
    x0 = jnp.where(m0, x, 0.0)
    return jnp.concatenate([x0, x - x0], axis=-2)


def _rwkv_prep(r, k, v, wl, al, g, w0, a0, k_k, k_a, r_k):
    C = RW_CHUNK
    H = RW_HEAD
    P = r.shape[0]
    lane = lax.broadcasted_iota(jnp.int32, (1, 1, LANES), 2)
    m0 = lane < H

    log_decay = -DECAY_SCALE * _sigmoid(w0 + wl)
    a = _sigmoid(a0 + al)
    kk = k * k_k
    kk = kk * lax.rsqrt(jnp.maximum(_head_sum(kk * kk, m0), 1e-24))
    k2 = k * (1.0 + (a - 1.0) * k_a)
    b = kk * a

    ti = lax.broadcasted_iota(jnp.int32, (P, C, C), 1)
    si = lax.broadcasted_iota(jnp.int32, (P, C, C), 2)
    tri = jnp.where(ti >= si, 1.0, 0.0).astype(BF16)
    p_hi = log_decay.astype(BF16)
    rem = log_decay - p_hi.astype(F32)
    p_mid = rem.astype(BF16)
    p_lo = (rem - p_mid.astype(F32)).astype(BF16)
    cum3 = _bdot(tri, jnp.concatenate([p_hi, p_mid, p_lo], axis=2))
    cum = cum3[:, :, :LANES] + cum3[:, :, LANES:2 * LANES] + cum3[:, :, 2 * LANES:]

    mid = cum[:, C // 2 - 1:C // 2, :]
    last = cum[:, C - 1:C, :]
    e_fwd = jnp.exp(cum - mid)
    e_inv = jnp.exp(mid - cum)
    e_prev = jnp.exp(cum - log_decay - mid)
    d_mid = jnp.exp(mid)
    d_end_mid = jnp.exp(last - mid)
    d_end = jnp.exp(last)

    kk_t = kk * e_prev
    r_t = r * e_fwd
    b_t = b * e_inv
    k_t = k2 * e_inv
    bonus = _head_sum(r * k2 * r_k, m0) * v
    return kk_t, r_t, b_t, k_t, v, d_mid, d_end_mid, d_end, bonus, g


def _rwkv_chain(prep, lnx_w, lnx_b, S):
    kk_t, r_t, b_t, k_t, v, d_mid, d_end_mid, d_end, bonus, g = prep
    C = RW_CHUNK
    H = RW_HEAD
    lane = lax.broadcasted_iota(jnp.int32, (1, 1, LANES), 2)
    m0 = lane < H
    q = jnp.concatenate([kk_t, r_t], axis=1)
    q0 = jnp.where(m0, q, 0.0)
    g0 = _bdot_nt(q0, jnp.concatenate([b_t, k_t], axis=1))
    g1 = _bdot_nt(q - q0, jnp.concatenate([k_t, b_t], axis=1))

    t_row = lax.broadcasted_iota(jnp.int32, (1, C, LANES), 1)
    s_col = lax.broadcasted_iota(jnp.int32, (1, C, LANES), 2) & (C - 1)
    strict = t_row > s_col
    incl = t_row >= s_col
    g0t, g0b, g1t, g1b = g0[:, :C], g0[:, C:], g1[:, :C], g1[:, C:]
    l_cat = jnp.where(strict, jnp.where(m0, g0t, g1t), 0.0)
    ak_cat = jnp.where(strict, jnp.where(m0, g1t, g0t), 0.0)
    gb0 = jnp.where(incl, g0b, 0.0)
    gb1 = jnp.where(incl, g1b, 0.0)
    grb_cat = jnp.where(m0, gb0, gb1)

    rr = lax.broadcasted_iota(jnp.int32, (1, LANES, LANES), 1)
    cc = lax.broadcasted_iota(jnp.int32, (1, LANES, LANES), 2)
    bd_mask = (rr < H) == (cc < H)

    def block_diag(x_cat):
        return jnp.where(bd_mask, jnp.concatenate([x_cat, x_cat], axis=1), 0.0)

    t_cat = jnp.where(t_row == s_col, 1.0, 0.0) - l_cat
    m_cat = l_cat
    n = 1
    while 2 * n < C:
        m_cat = _bdot(m_cat, block_diag(m_cat))
        t_cat = t_cat + _bdot(t_cat, block_diag(m_cat))
        n *= 2

    v0 = jnp.where(m0, v, 0.0)
    v1 = v - v0
    kk_h = _bdot(t_cat, _split_heads(kk_t, m0))
    av = _bdot(ak_cat, jnp.concatenate([v1, v0], axis=1))
    u0 = -_bdot(t_cat, _split_heads(av, m0))
    r_h = r_t - _bdot(grb_cat, _split_heads(kk_h, m0))
    u00 = jnp.where(m0, u0, 0.0)
    y0 = _bdot(jnp.concatenate([gb0, gb1], axis=2),
               jnp.concatenate([u00, v0, v1, u0 - u00], axis=1))
    b_end = b_t * d_end_mid
    k_end = k_t * d_end_mid
    s_add = jnp.where(bd_mask, _bdot_tn(jnp.concatenate([u0, v], axis=1),
                                        jnp.concatenate([b_end, k_end], axis=1)), 0.0)
    kb = jnp.where(bd_mask, _bdot_tn(kk_h, b_end), 0.0)

    s_mid = S * d_mid
    y = _bdot_nt(r_h, s_mid) + y0
    s_new = S * d_end - _bdot(s_mid, kb) + s_add

    inv_h = 1.0 / H
    mu = _head_sum(y, m0) * inv_h
    yc = y - mu
    var = _head_sum(yc * yc, m0) * inv_h
    yn = yc * lax.rsqrt(var + GN_EPS) * lnx_w + lnx_b
    return (yn + bonus) * g, s_new


def _rwkv_chunk_body(r_ref, k_ref, v_ref, wl_ref, al_ref, g_ref,
                     w0_ref, a0_ref, kk_ref, ka_ref, rk_ref, lw_ref, lb_ref,
                     y_ref, s_out_ref, s_ref, *, n_pairs, n_sub, n_chunks):
    c = pl.program_id(2)

    @pl.when(c == 0)
    def _():
        s_ref[...] = jnp.zeros_like(s_ref)

    C = RW_CHUNK

    def pairs(ref, lead):
        return jnp.stack([ref[lead + (slice(p * LANES, (p + 1) * LANES),)]
                          for p in range(n_pairs)], axis=0)

    par = [pairs(ref, (slice(None),))
           for ref in (w0_ref, a0_ref, kk_ref, ka_ref, rk_ref, lw_ref, lb_ref)]

    def prep(j):
        rows = (0, slice(j * C, (j + 1) * C))
        tok = [pairs(ref, rows) for ref in (r_ref, k_ref, v_ref, wl_ref, al_ref, g_ref)]
        return _rwkv_prep(*tok, *par[:5])

    S = s_ref[...]
    nxt = prep(0)
    for j in range(n_sub):
        cur = nxt
        if j + 1 < n_sub:
            nxt = prep(j + 1)
        yg, S = _rwkv_chain(cur, par[5], par[6], S)
        for p in range(n_pairs):
            y_ref[0, j * C:(j + 1) * C, p * LANES:(p + 1) * LANES] = yg[p].astype(y_ref.dtype)
    s_ref[...] = S

    @pl.when(c == n_chunks - 1)
    def _():
        s_out_ref[0] = S


def _rwkv_prompt(r, k, v, wl, al, g, w0, a0, k_k, k_a, r_k, lnx_w, lnx_b):
    B, T, D = r.shape
    LW = RW_LANES_PER_STEP
    rows = RW_CHUNK * RW_SUBCHUNKS
    n_pairs = LW // LANES
    n_chunks = T // rows
    tok = pl.BlockSpec((1, rows, LW), lambda b, h, c: (b, c, h))
    par = pl.BlockSpec((1, LW), lambda b, h, c: (0, h))
    body = functools.partial(_rwkv_chunk_body, n_pairs=n_pairs, n_sub=RW_SUBCHUNKS,
                             n_chunks=n_chunks)
    y, s_bd = pl.pallas_call(
        body,
        grid=(B, D // LW, n_chunks),
        in_specs=[tok] * 6 + [par] * 7,
        out_specs=[pl.BlockSpec((1, rows, LW), lambda b, h, c: (b, c, h)),
                   pl.BlockSpec((1, n_pairs, LANES, LANES), lambda b, h, c: (b, h, 0, 0))],
        out_shape=[jax.ShapeDtypeStruct((B, T, D), BF16),
                   jax.ShapeDtypeStruct((B, D // LANES, LANES, LANES), F32)],
        scratch_shapes=[pltpu.VMEM((n_pairs, LANES, LANES), F32)],
        compiler_params=_cparams(("parallel", "parallel", "arbitrary")),
        name="rwkv_chunk",
    )(r, k, v, wl, al, g, *[p.reshape(1, D) for p in (w0, a0, k_k, k_a, r_k, lnx_w, lnx_b)])
    H = RW_HEAD
    s = jnp.stack([s_bd[:, :, :H, :H], s_bd[:, :, H:, H:]], axis=2)
    return y, s.reshape(B, D // H, H, H)


def _rwkv_step_body(s_ref, d_ref, kk_ref, b_ref, k_ref, r_ref, v_ref, s_out_ref, y_ref):
    S = s_ref[0]
    s_kk = jnp.sum(S * kk_ref[0], axis=-1, keepdims=True)
    S = S * d_ref[0] - s_kk * b_ref[0] + v_ref[0] * k_ref[0]
    s_out_ref[0] = S
    y_ref[0] = jnp.sum(S * r_ref[0], axis=-1, keepdims=True)


def _rwkv_step(S0, decay, kk, b, k, r, v):
    B, Hh, N, _ = S0.shape
    row = lambda t: t.reshape(B, Hh, 1, N)
    col = lambda t: t.reshape(B, Hh, N, 1)
    row_spec = pl.BlockSpec((1, Hh, 1, N), lambda i: (i, 0, 0, 0))
    col_spec = pl.BlockSpec((1, Hh, N, 1), lambda i: (i, 0, 0, 0))
    st_spec = pl.BlockSpec((1, Hh, N, N), lambda i: (i, 0, 0, 0))
    S1, y = pl.pallas_call(
        _rwkv_step_body,
        grid=(B,),
        in_specs=[st_spec] + [row_spec] * 5 + [col_spec],
        out_specs=[st_spec, col_spec],
        out_shape=[jax.ShapeDtypeStruct(S0.shape, F32), jax.ShapeDtypeStruct((B, Hh, N, 1), F32)],
        compiler_params=_cparams(("parallel",)),
        name="rwkv_step",
    )(S0, row(decay), row(kk), row(b), row(k), row(r), col(v))
    return S1, y.reshape(B, Hh * N)


def _swa_prompt_body(q_ref, kp_ref, kc_ref, vp_ref, vc_ref, qg_ref, sink_ref, o_ref):
    i = pl.program_id(1)
    W = WINDOW
    lane = lax.broadcasted_iota(jnp.int32, (1, LANES), 1)
    m0 = lane < HEAD_DIM
    a_idx = lax.broadcasted_iota(jnp.int32, (W, 2 * W), 0)
    c_idx = lax.broadcasted_iota(jnp.int32, (W, 2 * W), 1)
    first_key = jnp.where(i > 0, 0, W)
    mask = (c_idx > a_idx) & (c_idx <= a_idx + W) & (c_idx >= first_key)
    mask_all = jnp.concatenate([mask] * GQA_GROUP, axis=0)
    neg = jnp.float32(-jnp.inf)
    qg = qg_ref[...]

    for blk in range(N_KV_HEADS // 2):
        ksl = slice(blk * LANES, (blk + 1) * LANES)
        kb = jnp.concatenate([kp_ref[0, :, ksl], kc_ref[0, :, ksl]], axis=0)
        vb = jnp.concatenate([vp_ref[0, :, ksl], vc_ref[0, :, ksl]], axis=0)
        kb_r = pltpu.roll(kb, HEAD_DIM, 1)
        vb_r = pltpu.roll(vb, HEAD_DIM, 1)
        for sub in range(2):
            if sub == 0:
                k2 = jnp.where(m0, kb, kb_r)
                v2 = jnp.where(m0, vb, vb_r)
            else:
                k2 = jnp.where(m0, kb_r, kb)
                v2 = jnp.where(m0, vb_r, vb)
            k2 = k2.astype(BF16)
            v2 = v2.astype(BF16)
            kv = 2 * blk + sub
            n_pairs = GQA_GROUP // 2
            q_rows, sink_rows = [], []
            for pp in range(n_pairs):
                pair = kv * n_pairs + pp
                qsl = slice(pair * LANES, (pair + 1) * LANES)
                qv = q_ref[0, :, qsl]
                ms = _head_sum(qv * qv, m0) * (1.0 / HEAD_DIM)
                qn = qv * lax.rsqrt(ms + RMS_EPS) * qg * ATTN_SCALE
                q0 = jnp.where(m0, qn, 0.0)
                q_rows += [q0, qn - q0]
                sv = sink_ref[:, qsl]
                for hm in (m0, jnp.logical_not(m0)):
                    sink = jnp.max(jnp.where(hm, sv, neg), axis=-1, keepdims=True)
                    sink_rows.append(jnp.broadcast_to(sink, (W, LANES)))
            q_all = jnp.concatenate(q_rows, axis=0).astype(BF16)
            sink = jnp.concatenate(sink_rows, axis=0)
            s = lax.dot_general(q_all, k2, (((1,), (1,)), ((), ())), preferred_element_type=F32)
            s = jnp.where(mask_all, s, neg)
            s_a, s_b = s[:, :LANES], s[:, LANES:]
            row_max = jnp.max(jnp.maximum(s_a, s_b), axis=-1, keepdims=True)
            m = jnp.maximum(row_max, sink)
            p_a = jnp.exp(s_a - m)
            p_b = jnp.exp(s_b - m)
            row_sum = jnp.sum(p_a + p_b, axis=-1, keepdims=True)
            inv = 1.0 / (row_sum + jnp.exp(sink - m))
            p = jnp.concatenate([p_a * inv, p_b * inv], axis=1)
            o = jnp.dot(p.astype(BF16), v2, preferred_element_type=F32)
            for pp in range(n_pairs):
                pair = kv * n_pairs + pp
                o0 = o[(2 * pp) * W:(2 * pp + 1) * W]
                o1 = o[(2 * pp + 1) * W:(2 * pp + 2) * W]
                o_ref[0, :, pair * LANES:(pair + 1) * LANES] = jnp.where(m0, o0, o1).astype(o_ref.dtype)


def _swa_prompt(q, k, v, q_norm_g, sinks):
    B, T, D = q.shape
    KVW = k.shape[-1]
    W = WINDOW
    qg = jnp.tile(q_norm_g.astype(F32), 2).reshape(1, LANES)
    sink_l = jnp.repeat(sinks.astype(F32), HEAD_DIM).reshape(1, D)
    cur = pl.BlockSpec((1, W, KVW), lambda b, i: (b, i, 0))
    prev = pl.BlockSpec((1, W, KVW), lambda b, i: (b, jnp.maximum(i - 1, 0), 0))
    return pl.pallas_call(
        _swa_prompt_body,
        grid=(B, T // W),
        in_specs=[pl.BlockSpec((1, W, D), lambda b, i: (b, i, 0)), prev, cur, prev, cur,
                  pl.BlockSpec((1, LANES), lambda b, i: (0, 0)),
                  pl.BlockSpec((1, D), lambda b, i: (0, 0))],
        out_specs=pl.BlockSpec((1, W, D), lambda b, i: (b, i, 0)),
        out_shape=jax.ShapeDtypeStruct((B, T, D), BF16),
        compiler_params=_cparams(("parallel", "parallel")),
        name="swa_prompt",
    )(q, k, k, v, v, qg, sink_l)


def _swa_sample_body(q_ref, kc_ref, vc_ref, kn_ref, vn_ref, qg_ref, sink_ref, o_ref):
    G, HD = GQA_GROUP, HEAD_DIM
    q = q_ref[0]
    ms = jnp.mean(q * q, axis=-1, keepdims=True)
    qn = q * lax.rsqrt(ms + RMS_EPS) * qg_ref[...] * ATTN_SCALE
    n_buf = kc_ref.shape[1]
    col = lax.broadcasted_iota(jnp.int32, (G, n_buf), 1)
    valid = (n_buf - col) < WINDOW
    neg = jnp.float32(-jnp.inf)
    for kv in range(N_KV_HEADS):
        qj = qn[kv * G:(kv + 1) * G, :]
        lsl = slice(kv * HD, (kv + 1) * HD)
        kc = kc_ref[0, :, lsl]
        vc = vc_ref[0, :, lsl]
        kn = kn_ref[0, :, lsl]
        vn = vn_ref[0, :, lsl]
        s_c = jnp.where(valid, _dot_nt(qj, kc), neg)
        s_n = jnp.sum(qj.astype(BF16).astype(F32) * kn.astype(BF16).astype(F32),
                      axis=-1, keepdims=True)
        sink = sink_ref[kv * G:(kv + 1) * G, :]
        m = jnp.maximum(jnp.maximum(jnp.max(s_c, axis=-1, keepdims=True), s_n), sink)
        p_c = jnp.exp(s_c - m)
        p_n = jnp.exp(s_n - m)
        denom = jnp.sum(p_c, axis=-1, keepdims=True) + p_n + jnp.exp(sink - m)
        p_c = p_c / denom
        p_n = p_n / denom
        o = _dot(p_c, vc) + p_n.astype(BF16).astype(F32) * vn.astype(BF16).astype(F32)
        o_ref[0, kv * G:(kv + 1) * G, :] = o.astype(o_ref.dtype)


def _swa_sample(q, k_buf, v_buf, k_new, v_new, q_norm_g, sinks):
    B, D = q.shape
    n_buf, KVW = k_buf.shape[1:]
    NH, HD = N_Q_HEADS, HEAD_DIM
    buf = pl.BlockSpec((1, n_buf, KVW), lambda b: (b, 0, 0))
    new = pl.BlockSpec((1, 1, KVW), lambda b: (b, 0, 0))
    o = pl.pallas_call(
        _swa_sample_body,
        grid=(B,),
        in_specs=[pl.BlockSpec((1, NH, HD), lambda b: (b, 0, 0)), buf, buf, new, new,
                  pl.BlockSpec((1, HD), lambda b: (0, 0)),
                  pl.BlockSpec((NH, 1), lambda b: (0, 0))],
        out_specs=pl.BlockSpec((1, NH, HD), lambda b: (b, 0, 0)),
        out_shape=jax.ShapeDtypeStruct((B, NH, HD), BF16),
        compiler_params=_cparams(("parallel",)),
        name="swa_sample",
    )(q.reshape(B, NH, HD), k_buf, v_buf, k_new, v_new,
      q_norm_g.astype(F32).reshape(1, HD), sinks.astype(F32).reshape(NH, 1))
    return o.reshape(B, D)


def _rms(x, g):
    return x * lax.rsqrt(jnp.mean(x * x, axis=-1, keepdims=True) + RMS_EPS) * g


def _silu(x):
    return x * jax.nn.sigmoid(x)


def _forward(x, mods, kvmod, wkv0, shift0, conv0, k_buf, v_buf, p, wsrc, tag):
    B, T, D = x.shape
    M = B * T
    depth = len(mods)
    n_a = depth // 2
    emit = T == 1
    wb = {}

    def mm(xin, wname, l=None, **kw):
        if emit:
            out, w16 = _mm(xin, wsrc[wname], layer=l, emit_w=True, name=f"{tag}_{wname}", **kw)
            wb[(wname, l)] = w16
            return out
        w = wsrc[wname] if l is None else wsrc[wname][l]
        return _mm(xin, w, name=f"{tag}_{wname}", **kw)

    new_wkv, new_shift, new_conv = [], [], []
    k_att = v_att = k_win = v_win = None
    x2 = x.reshape(M, D)
    for l in range(depth):
        sh1, sc1, gt1, sh2, sc2, gt2 = jnp.split(mods[l], 6, axis=-1)
        x3 = x2.reshape(B, T, D)
        if l < n_a:
            mixes, h_last = _norm_shift(x3, p['ln1_g'][l], sc1, sh1, shift0[l], p['rwkv_mix'][l])
            xr, xw, xk, xv, xa, xg = [m.reshape(M, D) for m in mixes]
            r = mm(xr, 'rwkv_w_r', l)
            k = mm(xk, 'rwkv_w_k', l)
            v = mm(xv, 'rwkv_w_v', l)
            wl = mm(mm(xw, 'rwkv_w1', l, act="tanh", out_dtype=BF16), 'rwkv_w2', l)
            al = mm(mm(xa, 'rwkv_a1', l, out_dtype=BF16), 'rwkv_a2', l)
            g = mm(mm(xg, 'rwkv_g1', l, act="sigmoid", out_dtype=BF16), 'rwkv_g2', l)
            w0, a0 = p['rwkv_w0'][l], p['rwkv_a0'][l]
            k_k, k_a = p['rwkv_k_k'][l], p['rwkv_k_a'][l]
            r_k = p['rwkv_r_k'][l].reshape(D)
            lnx_w, lnx_b = p['rwkv_lnx_w'][l], p['rwkv_lnx_b'][l]
            if T > 1:
                sh = (B, T, D)
                yg, S = _rwkv_prompt(r.reshape(sh), k.reshape(sh), v.reshape(sh), wl.reshape(sh),
                                     al.reshape(sh), g.reshape(sh), w0, a0, k_k, k_a, r_k,
                                     lnx_w, lnx_b)
                yg = yg.reshape(M, D)
            else:
                Hh, N = RW_HEADS, RW_HEAD
                w_log = -jax.nn.softplus(-(w0 + wl)) - 0.5
                decay = jnp.exp(-jnp.exp(w_log))
                a = jax.nn.sigmoid(a0 + al)
                kk = (k * k_k).reshape(B, Hh, N)
                kk = kk / jnp.maximum(jnp.sqrt(jnp.sum(kk * kk, axis=-1, keepdims=True)), 1e-12)
                kk = kk.reshape(B, D)
                k2 = k * (1.0 + (a - 1.0) * k_a)
                S, y = _rwkv_step(wkv0[l].astype(F32), decay, kk, kk * a, k2, r, v)
                y4 = y.reshape(B, Hh, N)
                mu = jnp.mean(y4, axis=-1, keepdims=True)
                var = jnp.mean(jnp.square(y4 - mu), axis=-1, keepdims=True)
                y4 = ((y4 - mu) * lax.rsqrt(var + GN_EPS) * lnx_w.reshape(Hh, N)
                      + lnx_b.reshape(Hh, N))
                bonus = jnp.sum((r * k2 * r_k).reshape(B, Hh, N), axis=-1, keepdims=True)
                y4 = y4 + bonus * v.reshape(B, Hh, N)
                yg = (y4.reshape(B, D) * g).astype(BF16)
            new_wkv.append(S)
            new_shift.append(h_last)
            x2 = mm(yg, 'rwkv_w_o', l, res=x2, gate=gt1, rows_per_gate=T)
        else:
            j = l - n_a
            h = _norm_mod(x3, p['ln1_g'][l], sc1, sh1).reshape(M, D)
            q = mm(h, 'attn_w_q', j)
            if k_buf is None:
                attn = _swa_prompt(q.reshape(B, T, D), k_att, v_att, p['attn_q_norm_g'][j],
                                   p['attn_sinks'][j]).reshape(M, D)
            else:
                attn = _swa_sample(q, k_buf, v_buf, k_att, v_att, p['attn_q_norm_g'][j],
                                   p['attn_sinks'][j])
            x2 = mm(attn, 'attn_w_o', j, res=x2, gate=gt1, rows_per_gate=T)
        h2 = _norm_mod(x2.reshape(B, T, D), p['ln2_g'][l], sc2, sh2).reshape(M, D)
        if emit:
            act, cb, wg16, wu16 = _ffn_in_tok(h2, wsrc['ffn_w_in'], l, p['ffn_conv_w'][l],
                                              p['ffn_conv_b'][l], conv0[l])
            wb[('ffn_w_gate', l)] = wg16
            wb[('ffn_w_up', l)] = wu16
        else:
            act, cb = _ffn_in_seq(h2, wsrc['ffn_w_gate'][l], wsrc['ffn_w_up'][l],
                                  p['ffn_conv_w'][l], p['ffn_conv_b'][l], conv0[l], T)
        new_conv.append(cb)
        x2 = mm(act, 'ffn_w_out', l, res=x2, gate=gt2, rows_per_gate=T)
        if l == n_a - 1:
            sh, sc = jnp.split(kvmod, 2, axis=-1)
            hn = _norm_mod(x2.reshape(B, T, D), p['kv_norm_g'], sc, sh).reshape(M, D)
            kv = mm(hn, 'w_kv')
            KVW = kv.shape[-1] // 2
            k_new = _rms(kv[:, :KVW].reshape(B, T, N_KV_HEADS, HEAD_DIM), p['k_norm_g'])
            v_new = kv[:, KVW:].reshape(B, T, N_KV_HEADS, HEAD_DIM)
            k_att, v_att = k_new.reshape(B, T, KVW), v_new.reshape(B, T, KVW)
            if k_buf is None:
                w = min(WINDOW, T)
                k_win, v_win = k_new[:, T - w:], v_new[:, T - w:]
            else:
                n_buf = k_buf.shape[1]
                k_win = jnp.concatenate([k_buf, k_new], axis=1)[:, -n_buf:]
                v_win = jnp.concatenate([v_buf, v_new], axis=1)[:, -n_buf:]
                k_buf = k_buf.reshape(B, n_buf, KVW)
                v_buf = v_buf.reshape(B, n_buf, KVW)
    outs = (x2.reshape(B, T, D), jnp.stack(new_wkv), jnp.stack(new_shift), jnp.stack(new_conv),
            k_win, v_win)
    return outs, wb


class _Layered:
    def __init__(self, wb):
        self._wb = wb

    def __getitem__(self, name):
        if (name, None) in self._wb:
            return self._wb[(name, None)]
        n = 1 + max(l for (nm, l) in self._wb if nm == name)
        return [self._wb[(name, l)] for l in range(n)]


def kernel(x_prompt, x_sample, c_prompt, c_sample, state_wkv, state_shift, state_conv, cache_k_win, cache_v_win, mod_w, mod_b, ln1_g, ln2_g, rwkv_mix, rwkv_w0, rwkv_w1, rwkv_w2, rwkv_a0, rwkv_a1, rwkv_a2, rwkv_g1, rwkv_g2, rwkv_k_k, rwkv_k_a, rwkv_r_k, rwkv_w_r, rwkv_w_k, rwkv_w_v, rwkv_w_o, rwkv_lnx_w, rwkv_lnx_b, kv_norm_g, kv_mod_w, kv_mod_b, w_kv, k_norm_g, attn_w_q, attn_q_norm_g, attn_sinks, attn_w_o, ffn_w_in, ffn_conv_w, ffn_conv_b, ffn_w_out):
    p = dict(mod_w=mod_w, mod_b=mod_b, ln1_g=ln1_g, ln2_g=ln2_g,
             rwkv_mix=rwkv_mix, rwkv_w0=rwkv_w0, rwkv_w1=rwkv_w1, rwkv_w2=rwkv_w2,
             rwkv_a0=rwkv_a0, rwkv_a1=rwkv_a1, rwkv_a2=rwkv_a2, rwkv_g1=rwkv_g1, rwkv_g2=rwkv_g2,
             rwkv_k_k=rwkv_k_k, rwkv_k_a=rwkv_k_a, rwkv_r_k=rwkv_r_k, rwkv_w_r=rwkv_w_r,
             rwkv_w_k=rwkv_w_k, rwkv_w_v=rwkv_w_v, rwkv_w_o=rwkv_w_o,
             rwkv_lnx_w=rwkv_lnx_w, rwkv_lnx_b=rwkv_lnx_b,
             kv_norm_g=kv_norm_g, kv_mod_w=kv_mod_w, kv_mod_b=kv_mod_b, w_kv=w_kv, k_norm_g=k_norm_g,
             attn_w_q=attn_w_q, attn_q_norm_g=attn_q_norm_g, attn_sinks=attn_sinks, attn_w_o=attn_w_o,
             ffn_w_in=ffn_w_in, ffn_conv_w=ffn_conv_w, ffn_conv_b=ffn_conv_b, ffn_w_out=ffn_w_out)
    depth = mod_w.shape[0]
    n_a = depth // 2
    Bp = x_prompt.shape[0]
    dt = x_prompt.dtype

    c_all = _silu(jnp.concatenate([c_prompt, c_sample], axis=0)).astype(BF16)
    mods = [_mm(c_all, mod_w, layer=l, bias=mod_b[l], name=f"mod{l}") for l in range(depth)]
    kvmod = _mm(c_all, kv_mod_w, bias=kv_mod_b, name="kvmod")

    outs_s, wb = _forward(x_sample, [m[Bp:] for m in mods], kvmod[Bp:], state_wkv, state_shift,
                          state_conv, cache_k_win, cache_v_win, p, p, "s")
    F = ffn_conv_b.shape[-1]
    wkv0 = jnp.zeros((n_a, Bp, RW_HEADS, RW_HEAD, RW_HEAD), dt)
    shift0 = jnp.zeros((n_a, Bp, D_MODEL), dt)
    conv0 = jnp.zeros((depth, Bp, CONV_W - 1, F), dt)
    outs_p, _ = _forward(x_prompt, [m[:Bp] for m in mods], kvmod[:Bp], wkv0, shift0, conv0,
                         None, None, p, _Layered(wb), "p")
    y_p, wkv_p, shift_p, conv_p, kwin_p, vwin_p = outs_p
    y_s, wkv_s, shift_s, conv_s, kwin_s, vwin_s = outs_s
    return (y_p, y_s, wkv_p, wkv_s, shift_p, shift_s, conv_p, conv_s,
            kwin_p, kwin_s, vwin_p, vwin_s)
```

```python
import functools

import jax
import jax.numpy as jnp
from jax import lax
from jax.experimental import pallas as pl
from jax.experimental.pallas import tpu as pltpu

F32 = jnp.float32
BF16 = jnp.bfloat16

D_MODEL = 4096
RW_HEAD = 64
RW_HEADS = D_MODEL // RW_HEAD
GN_EPS = 64e-5
DECAY_SCALE = 0.6065306597126334
RMS_EPS = 1e-6
HEAD_DIM = 64
N_Q_HEADS = D_MODEL // HEAD_DIM
N_KV_HEADS = N_Q_HEADS // 8
GQA_GROUP = N_Q_HEADS // N_KV_HEADS
WINDOW = 128
ATTN_SCALE = HEAD_DIM ** -0.5
CONV_W = 3

LANES = 128
SUBLANES = 8
MXU_COLS = 256
VMEM_LIMIT_BYTES = 56 * 1024 * 1024

RW_CHUNK = 64
RW_LANES_PER_STEP = 2048
RW_SUBCHUNKS = 4
NORM_ROWS = 256


def _cparams(sem):
    return pltpu.CompilerParams(dimension_semantics=sem, vmem_limit_bytes=VMEM_LIMIT_BYTES)


def _pick_tile(dim, prefs):
    for p in prefs:
        if dim >= p and dim % p == 0:
            return p
    return dim


def _sigmoid(x):
    return 1.0 / (1.0 + jnp.exp(-x))


def _mm_body(*refs, nk, has_bias, act, has_res, emit_w):
    x_ref, w_ref = refs[0], refs[1]
    pos = 2
    bias_ref = res_ref = gate_ref = wout_ref = None
    if has_bias:
        bias_ref = refs[pos]; pos += 1
    if has_res:
        res_ref, gate_ref = refs[pos], refs[pos + 1]; pos += 2
    o_ref = refs[pos]; pos += 1
    if emit_w:
        wout_ref = refs[pos]; pos += 1
    acc_ref = refs[pos] if nk > 1 else None

    w = w_ref[...].astype(BF16)
    if emit_w:
        wout_ref[...] = w
    part = jnp.dot(x_ref[...].astype(BF16), w, preferred_element_type=F32)

    def finish(acc):
        if has_bias:
            acc = acc + bias_ref[...]
        if act == "tanh":
            acc = jnp.tanh(acc)
        elif act == "sigmoid":
            acc = _sigmoid(acc)
        if has_res:
            acc = res_ref[...] + gate_ref[...] * acc
        o_ref[...] = acc.astype(o_ref.dtype)

    if nk == 1:
        finish(part)
    else:
        k = pl.program_id(2)

        @pl.when(k == 0)
        def _():
            acc_ref[...] = part

        @pl.when(k > 0)
        def _():
            acc_ref[...] += part

        @pl.when(k == nk - 1)
        def _():
            finish(acc_ref[...])


def _mm(x, w, *, layer=None, bias=None, act=None, res=None, gate=None, rows_per_gate=1,
        out_dtype=F32, emit_w=False, name="mm"):
    M, K = x.shape
    K2, N = w.shape[-2:]
    assert K == K2 and (w.ndim == 2) == (layer is None)
    tm = _pick_tile(M, (1024,))
    tn = _pick_tile(N, (1024, 512))
    tk = _pick_tile(K, (2048,)) if M < 1024 or K <= 2048 else (K if K <= 4096 else
                                                                _pick_tile(K, (3584, 2048)))
    nk = K // tk
    grid = (M // tm, N // tn, nk)
    assert not emit_w or grid[0] == 1

    if layer is None:
        w_spec = pl.BlockSpec((tk, tn), lambda i, j, k: (k, j))
    else:
        w_spec = pl.BlockSpec((None, tk, tn), lambda i, j, k: (layer, k, j))
    in_specs = [pl.BlockSpec((tm, tk), lambda i, j, k: (i, k)), w_spec]
    args = [x, w]
    if bias is not None:
        in_specs.append(pl.BlockSpec((1, tn), lambda i, j, k: (0, j)))
        args.append(bias.reshape(1, N).astype(F32))
    if res is not None:
        in_specs.append(pl.BlockSpec((tm, tn), lambda i, j, k: (i, j)))
        args.append(res)
        if rows_per_gate == 1:
            in_specs.append(pl.BlockSpec((tm, tn), lambda i, j, k: (i, j)))
            args.append(gate)
        else:
            assert rows_per_gate % tm == 0
            bpg = rows_per_gate // tm
            in_specs.append(pl.BlockSpec((None, 1, tn), lambda i, j, k: (i // bpg, 0, j)))
            args.append(gate.reshape(gate.shape[0], 1, N))
    out_specs = [pl.BlockSpec((tm, tn), lambda i, j, k: (i, j))]
    out_shape = [jax.ShapeDtypeStruct((M, N), out_dtype)]
    if emit_w:
        out_specs.append(pl.BlockSpec((tk, tn), lambda i, j, k: (k, j)))
        out_shape.append(jax.ShapeDtypeStruct((K, N), BF16))
    scratch = [pltpu.VMEM((tm, tn), F32)] if nk > 1 else []
    body = functools.partial(_mm_body, nk=nk, has_bias=bias is not None, act=act,
                             has_res=res is not None, emit_w=emit_w)
    outs = pl.pallas_call(
        body,
        grid=grid,
        in_specs=in_specs,
        out_specs=out_specs,
        out_shape=out_shape,
        scratch_shapes=scratch,
        compiler_params=_cparams(("parallel", "parallel", "arbitrary")),
        name=name,
    )(*args)
    return tuple(outs) if emit_w else outs[0]


def _accumulate(k, nk, pairs):
    if nk == 1:
        return

    @pl.when(k == 0)
    def _():
        for acc_ref, part in pairs:
            acc_ref[...] = part

    @pl.when(k > 0)
    def _():
        for acc_ref, part in pairs:
            acc_ref[...] += part


def _ffn_in_seq_body(x_ref, wg_ref, wu_ref, cw_ref, cb_ref, st_ref,
                     act_ref, st_out_ref, carry, *, tiles_per_seq, n_sub):
    i = pl.program_id(1)

    @pl.when(i % tiles_per_seq == 0)
    def _():
        carry[...] = st_ref[...]

    x = x_ref[...]
    tm = x.shape[0]
    ts = act_ref.shape[1] // n_sub
    row = lax.broadcasted_iota(jnp.int32, (SUBLANES, ts), 0)
    for s in range(n_sub):
        sl = slice(s * ts, (s + 1) * ts)
        g = jnp.dot(x, wg_ref[:, sl], preferred_element_type=F32)
        u = jnp.dot(x, wu_ref[:, sl], preferred_element_type=F32)
        tail = carry[:, sl]
        before2, before1 = tail[0:1], tail[1:2]
        r1 = pltpu.roll(g, 1, 0)
        r2 = pltpu.roll(g, 2, 0)
        head1 = jnp.where(row == 0, before1, r1[:SUBLANES])
        head2 = jnp.where(row == 0, before2, jnp.where(row == 1, before1, r2[:SUBLANES]))
        prev1 = jnp.concatenate([head1, r1[SUBLANES:]], axis=0)
        prev2 = jnp.concatenate([head2, r2[SUBLANES:]], axis=0)
        cw = cw_ref[:, sl]
        conv = cb_ref[:, sl] + cw[0:1] * prev2 + cw[1:2] * prev1 + cw[2:3] * g
        act_ref[:, sl] = (conv * _sigmoid(conv) * u).astype(act_ref.dtype)
        last = g[tm - (CONV_W - 1):, :]
        st_out_ref[:, sl] = last
        carry[:, sl] = last


def _ffn_in_seq(x, wg, wu, conv_w, conv_b, state, T):
    M, K = x.shape
    F = wg.shape[1]
    tm = _pick_tile(T, (1024,))
    tn = _pick_tile(F, (512,))
    tps = T // tm
    body = functools.partial(_ffn_in_seq_body, tiles_per_seq=tps, n_sub=tn // MXU_COLS)
    w_spec = pl.BlockSpec((K, tn), lambda j, i: (0, j))
    act, st = pl.pallas_call(
        body,
        grid=(F // tn, M // tm),
        in_specs=[pl.BlockSpec((tm, K), lambda j, i: (i, 0)),
                  w_spec, w_spec,
                  pl.BlockSpec((CONV_W, tn), lambda j, i: (0, j)),
                  pl.BlockSpec((1, tn), lambda j, i: (0, j)),
                  pl.BlockSpec((None, CONV_W - 1, tn), lambda j, i: (i // tps, 0, j))],
        out_specs=[pl.BlockSpec((tm, tn), lambda j, i: (i, j)),
                   pl.BlockSpec((None, CONV_W - 1, tn), lambda j, i: (i, 0, j))],
        out_shape=[jax.ShapeDtypeStruct((M, F), BF16),
                   jax.ShapeDtypeStruct((M // tm, CONV_W - 1, F), F32)],
        scratch_shapes=[pltpu.VMEM((CONV_W - 1, tn), F32)],
        compiler_params=_cparams(("parallel", "arbitrary")),
        name="ffn_in_seq",
    )(x, wg, wu, conv_w.astype(F32), conv_b.reshape(1, F).astype(F32), state.astype(F32))
    return act, st[tps - 1::tps]


def _ffn_in_tok_body(x_ref, wg_ref, wu_ref, cw_ref, cb_ref, s0_ref, s1_ref,
                     act_ref, g_ref, wgo_ref, wuo_ref, accg, accu, *, nk):
    k = pl.program_id(2)
    x = x_ref[...]
    wg = wg_ref[...].astype(BF16)
    wu = wu_ref[...].astype(BF16)
    wgo_ref[...] = wg
    wuo_ref[...] = wu
    pg = jnp.dot(x, wg, preferred_element_type=F32)
    pu = jnp.dot(x, wu, preferred_element_type=F32)
    _accumulate(k, nk, [(accg, pg), (accu, pu)])

    @pl.when(k == nk - 1)
    def _():
        g = accg[...] if nk > 1 else pg
        u = accu[...] if nk > 1 else pu
        cw = cw_ref[...]
        conv = cb_ref[...] + cw[0:1] * s0_ref[...] + cw[1:2] * s1_ref[...] + cw[2:3] * g
        act_ref[...] = (conv * _sigmoid(conv) * u).astype(act_ref.dtype)
        g_ref[...] = g


def _ffn_in_tok(x, w_in, layer, conv_w, conv_b, state):
    B, K = x.shape
    F = w_in.shape[-1] // 2
    tn = _pick_tile(F, (1024, 512))
    tk = _pick_tile(K, (2048,))
    nk = K // tk
    nj = F // tn
    body = functools.partial(_ffn_in_tok_body, nk=nk)
    row_spec = pl.BlockSpec((B, tn), lambda i, j, k: (0, j))
    wo_spec = pl.BlockSpec((tk, tn), lambda i, j, k: (k, j))
    s0 = state[:, 0, :].astype(F32)
    s1 = state[:, 1, :].astype(F32)
    act, g, wg, wu = pl.pallas_call(
        body,
        grid=(1, nj, nk),
        in_specs=[pl.BlockSpec((B, tk), lambda i, j, k: (0, k)),
                  pl.BlockSpec((None, tk, tn), lambda i, j, k: (layer, k, j)),
                  pl.BlockSpec((None, tk, tn), lambda i, j, k: (layer, k, j + nj)),
                  pl.BlockSpec((CONV_W, tn), lambda i, j, k: (0, j)),
                  pl.BlockSpec((1, tn), lambda i, j, k: (0, j)),
                  row_spec, row_spec],
        out_specs=[row_spec, row_spec, wo_spec, wo_spec],
        out_shape=[jax.ShapeDtypeStruct((B, F), BF16), jax.ShapeDtypeStruct((B, F), F32),
                   jax.ShapeDtypeStruct((K, F), BF16), jax.ShapeDtypeStruct((K, F), BF16)],
        scratch_shapes=[pltpu.VMEM((B, tn), F32), pltpu.VMEM((B, tn), F32)],
        compiler_params=_cparams(("parallel", "parallel", "arbitrary")),
        name="ffn_in_tok",
    )(x, w_in, w_in, conv_w.astype(F32), conv_b.reshape(1, F).astype(F32), s0, s1)
    return act, jnp.stack([s1, g], axis=1), wg, wu


def _norm_mod_rows(x, g, sc, sh):
    y = x * lax.rsqrt(jnp.mean(x * x, axis=-1, keepdims=True) + RMS_EPS) * g
    return y * (1.0 + sc) + sh


def _norm_body(x_ref, g_ref, sc_ref, sh_ref, o_ref):
    o_ref[0] = _norm_mod_rows(x_ref[0], g_ref[...], sc_ref[0], sh_ref[0]).astype(o_ref.dtype)


def _norm_mod(x, g, sc, sh):
    B, T, D = x.shape
    tr = _pick_tile(T, (2 * NORM_ROWS,))
    tok = pl.BlockSpec((1, tr, D), lambda b, t: (b, t, 0))
    per_b = pl.BlockSpec((1, 1, D), lambda b, t: (b, 0, 0))
    return pl.pallas_call(
        _norm_body,
        grid=(B, T // tr),
        in_specs=[tok, pl.BlockSpec((1, D), lambda b, t: (0, 0)), per_b, per_b],
        out_specs=tok,
        out_shape=jax.ShapeDtypeStruct((B, T, D), BF16),
        compiler_params=_cparams(("parallel", "parallel")),
        name="norm_mod",
    )(x, g.reshape(1, D), sc.reshape(B, 1, D), sh.reshape(B, 1, D))


def _norm_shift_body(x_ref, g_ref, sc_ref, sh_ref, s0_ref, mix_ref, *rest, n_mix):
    outs = rest[:n_mix]
    last_ref = rest[n_mix]
    carry = rest[n_mix + 1]
    t = pl.program_id(1)

    @pl.when(t == 0)
    def _():
        carry[...] = s0_ref[0]

    h = _norm_mod_rows(x_ref[0], g_ref[...], sc_ref[0], sh_ref[0])
    tr = h.shape[0]
    if tr == 1:
        h_prev = carry[...]
    else:
        row = lax.broadcasted_iota(jnp.int32, h.shape, 0)
        h_prev = jnp.where(row == 0, carry[...], pltpu.roll(h, 1, 0))
    dx = h_prev - h
    mix = mix_ref[...]
    for i in range(n_mix):
        outs[i][0] = (h + dx * mix[i:i + 1]).astype(outs[i].dtype)
    last = h[tr - 1:tr, :]
    carry[...] = last
    last_ref[0] = last


def _norm_shift(x, g, sc, sh, shift_prev, mix):
    B, T, D = x.shape
    n_mix = mix.shape[0]
    tr = _pick_tile(T, (NORM_ROWS,))
    tok = pl.BlockSpec((1, tr, D), lambda b, t: (b, t, 0))
    per_b = pl.BlockSpec((1, 1, D), lambda b, t: (b, 0, 0))
    body = functools.partial(_norm_shift_body, n_mix=n_mix)
    outs = pl.pallas_call(
        body,
        grid=(B, T // tr),
        in_specs=[tok, pl.BlockSpec((1, D), lambda b, t: (0, 0)), per_b, per_b, per_b,
                  pl.BlockSpec((n_mix, D), lambda b, t: (0, 0))],
        out_specs=[tok] * n_mix + [per_b],
        out_shape=[jax.ShapeDtypeStruct((B, T, D), BF16)] * n_mix
        + [jax.ShapeDtypeStruct((B, 1, D), F32)],
        scratch_shapes=[pltpu.VMEM((1, D), F32)],
        compiler_params=_cparams(("parallel", "arbitrary")),
        name="norm_shift",
    )(x, g.reshape(1, D), sc.reshape(B, 1, D), sh.reshape(B, 1, D),
      shift_prev.astype(F32).reshape(B, 1, D), mix.astype(F32))
    return outs[:n_mix], outs[n_mix].reshape(B, D)


def _bdot(a, b):
    return lax.dot_general(a.astype(BF16), b.astype(BF16), (((2,), (1,)), ((0,), (0,))),
                           preferred_element_type=F32)


def _bdot_nt(a, b):
    return lax.dot_general(a.astype(BF16), b.astype(BF16), (((2,), (2,)), ((0,), (0,))),
                           preferred_element_type=F32)


def _bdot_tn(a, b):
    return lax.dot_general(a.astype(BF16), b.astype(BF16), (((1,), (1,)), ((0,), (0,))),
                           preferred_element_type=F32)


def _dot(a, b):
    return jnp.dot(a.astype(BF16), b.astype(BF16), preferred_element_type=F32)


def _dot_nt(a, b):
    return lax.dot_general(a.astype(BF16), b.astype(BF16), (((1,), (1,)), ((), ())),
                           preferred_element_type=F32)


def _head_sum(x, m0):
    s0 = jnp.sum(jnp.where(m0, x, 0.0), axis=-1, keepdims=True)
    s1 = jnp.sum(jnp.where(m0, 0.0, x), axis=-1, keepdims=True)
    return jnp.where(m0, s0, s1)


def _split_heads(x, m0):
    x0 = jnp.where(m0, x, 0.0)
    return jnp.concatenate([x0, x - x0], axis=-2)


def _rwkv_prep(r, k, v, wl, al, g, w0, a0, k_k, k_a, r_k):
    C = RW_CHUNK
    H = RW_HEAD
    P = r.shape[0]
    lane = lax.broadcasted_iota(jnp.int32, (1, 1, LANES), 2)
    m0 = lane < H

    log_decay = -DECAY_SCALE * _sigmoid(w0 + wl)
    a = _sigmoid(a0 + al)
    kk = k * k_k
    kk = kk * lax.rsqrt(jnp.maximum(_head_sum(kk * kk, m0), 1e-24))
    k2 = k * (1.0 + (a - 1.0) * k_a)
    b = kk * a

    ti = lax.broadcasted_iota(jnp.int32, (P, C, C), 1)
    si = lax.broadcasted_iota(jnp.int32, (P, C, C), 2)
    tri = jnp.where(ti >= si, 1.0, 0.0).astype(BF16)
    p_hi = log_decay.astype(BF16)
    rem = log_decay - p_hi.astype(F32)
    p_mid = rem.astype(BF16)
    p_lo = (rem - p_mid.astype(F32)).astype(BF16)
    cum3 = _bdot(tri, jnp.concatenate([p_hi, p_mid, p_lo], axis=2))
    cum = cum3[:, :, :LANES] + cum3[:, :, LANES:2 * LANES] + cum3[:, :, 2 * LANES:]

    mid = cum[:, C // 2 - 1:C // 2, :]
    last = cum[:, C - 1:C, :]
    e_fwd = jnp.exp(cum - mid)
    e_inv = jnp.exp(mid - cum)
    e_prev = jnp.exp(cum - log_decay - mid)
    d_mid = jnp.exp(mid)
    d_end_mid = jnp.exp(last - mid)
    d_end = jnp.exp(last)

    kk_t = kk * e_prev
    r_t = r * e_fwd
    b_t = b * e_inv
    k_t = k2 * e_inv
    bonus = _head_sum(r * k2 * r_k, m0) * v
    return kk_t, r_t, b_t, k_t, v, d_mid, d_end_mid, d_end, bonus, g


def _rwkv_chain(prep, lnx_w, lnx_b, S):
    kk_t, r_t, b_t, k_t, v, d_mid, d_end_mid, d_end, bonus, g = prep
    C = RW_CHUNK
    H = RW_HEAD
    lane = lax.broadcasted_iota(jnp.int32, (1, 1, LANES), 2)
    m0 = lane < H
    q = jnp.concatenate([kk_t, r_t], axis=1)
    q0 = jnp.where(m0, q, 0.0)
    g0 = _bdot_nt(q0, jnp.concatenate([b_t, k_t], axis=1))
    g1 = _bdot_nt(q - q0, jnp.concatenate([k_t, b_t], axis=1))

    t_row = lax.broadcasted_iota(jnp.int32, (1, C, LANES), 1)
    s_col = lax.broadcasted_iota(jnp.int32, (1, C, LANES), 2) & (C - 1)
    strict = t_row > s_col
    incl = t_row >= s_col
    g0t, g0b, g1t, g1b = g0[:, :C], g0[:, C:], g1[:, :C], g1[:, C:]
    l_cat = jnp.where(strict, jnp.where(m0, g0t, g1t), 0.0)
    ak_cat = jnp.where(strict, jnp.where(m0, g1t, g0t), 0.0)
    gb0 = jnp.where(incl, g0b, 0.0)
    gb1 = jnp.where(incl, g1b, 0.0)
    grb_cat = jnp.where(m0, gb0, gb1)

    rr = lax.broadcasted_iota(jnp.int32, (1, LANES, LANES), 1)
    cc = lax.broadcasted_iota(jnp.int32, (1, LANES, LANES), 2)
    bd_mask = (rr < H) == (cc < H)

    def block_diag(x_cat):
        return jnp.where(bd_mask, jnp.concatenate([x_cat, x_cat], axis=1), 0.0)

    t_cat = jnp.where(t_row == s_col, 1.0, 0.0) - l_cat
    m_cat = l_cat
    n = 1
    while 2 * n < C:
        m_cat = _bdot(m_cat, block_diag(m_cat))
        t_cat = t_cat + _bdot(t_cat, block_diag(m_cat))
        n *= 2

    v0 = jnp.where(m0, v, 0.0)
    v1 = v - v0
    kk_h = _bdot(t_cat, _split_heads(kk_t, m0))
    av = _bdot(ak_cat, jnp.concatenate([v1, v0], axis=1))
    u0 = -_bdot(t_cat, _split_heads(av, m0))
    r_h = r_t - _bdot(grb_cat, _split_heads(kk_h, m0))
    u00 = jnp.where(m0, u0, 0.0)
    y0 = _bdot(jnp.concatenate([gb0, gb1], axis=2),
               jnp.concatenate([u00, v0, v1, u0 - u00], axis=1))
    b_end = b_t * d_end_mid
    k_end = k_t * d_end_mid
    s_add = jnp.where(bd_mask, _bdot_tn(jnp.concatenate([u0, v], axis=1),
                                        jnp.concatenate([b_end, k_end], axis=1)), 0.0)
    kb = jnp.where(bd_mask, _bdot_tn(kk_h, b_end), 0.0)

    s_mid = S * d_mid
    y = _bdot_nt(r_h, s_mid) + y0
    s_new = S * d_end - _bdot(s_mid, kb) + s_add

    inv_h = 1.0 / H
    mu = _head_sum(y, m0) * inv_h
    yc = y - mu
    var = _head_sum(yc * yc, m0) * inv_h
    yn = yc * lax.rsqrt(var + GN_EPS) * lnx_w + lnx_b
    return (yn + bonus) * g, s_new


def _rwkv_chunk_body(r_ref, k_ref, v_ref, wl_ref, al_ref, g_ref,
                     w0_ref, a0_ref, kk_ref, ka_ref, rk_ref, lw_ref, lb_ref,
                     y_ref, s_out_ref, s_ref, *, n_pairs, n_sub, n_chunks):
    c = pl.program_id(2)

    @pl.when(c == 0)
    def _():
        s_ref[...] = jnp.zeros_like(s_ref)

    C = RW_CHUNK

    def pairs(ref, lead):
        return jnp.stack([ref[lead + (slice(p * LANES, (p + 1) * LANES),)]
                          for p in range(n_pairs)], axis=0)

    par = [pairs(ref, (slice(None),))
           for ref in (w0_ref, a0_ref, kk_ref, ka_ref, rk_ref, lw_ref, lb_ref)]

    def prep(j):
        rows = (0, slice(j * C, (j + 1) * C))
        tok = [pairs(ref, rows) for ref in (r_ref, k_ref, v_ref, wl_ref, al_ref, g_ref)]
        return _rwkv_prep(*tok, *par[:5])

    S = s_ref[...]
    nxt = prep(0)
    for j in range(n_sub):
        cur = nxt
        if j + 1 < n_sub:
            nxt = prep(j + 1)
        yg, S = _rwkv_chain(cur, par[5], par[6], S)
        for p in range(n_pairs):
            y_ref[0, j * C:(j + 1) * C, p * LANES:(p + 1) * LANES] = yg[p].astype(y_ref.dtype)
    s_ref[...] = S

    @pl.when(c == n_chunks - 1)
    def _():
        s_out_ref[0] = S


def _rwkv_prompt(r, k, v, wl, al, g, w0, a0, k_k, k_a, r_k, lnx_w, lnx_b):
    B, T, D = r.shape
    LW = RW_LANES_PER_STEP
    rows = RW_CHUNK * RW_SUBCHUNKS
    n_pairs = LW // LANES
    n_chunks = T // rows
    tok = pl.BlockSpec((1, rows, LW), lambda b, h, c: (b, c, h))
    par = pl.BlockSpec((1, LW), lambda b, h, c: (0, h))
    body = functools.partial(_rwkv_chunk_body, n_pairs=n_pairs, n_sub=RW_SUBCHUNKS,
                             n_chunks=n_chunks)
    y, s_bd = pl.pallas_call(
        body,
        grid=(B, D // LW, n_chunks),
        in_specs=[tok] * 6 + [par] * 7,
        out_specs=[pl.BlockSpec((1, rows, LW), lambda b, h, c: (b, c, h)),
                   pl.BlockSpec((1, n_pairs, LANES, LANES), lambda b, h, c: (b, h, 0, 0))],
        out_shape=[jax.ShapeDtypeStruct((B, T, D), BF16),
                   jax.ShapeDtypeStruct((B, D // LANES, LANES, LANES), F32)],
        scratch_shapes=[pltpu.VMEM((n_pairs, LANES, LANES), F32)],
        compiler_params=_cparams(("parallel", "parallel", "arbitrary")),
        name="rwkv_chunk",
    )(r, k, v, wl, al, g, *[p.reshape(1, D) for p in (w0, a0, k_k, k_a, r_k, lnx_w, lnx_b)])
    H = RW_HEAD
    s = jnp.stack([s_bd[:, :, :H, :H], s_bd[:, :, H:, H:]], axis=2)
    return y, s.reshape(B, D // H, H, H)


def _rwkv_step_body(s_ref, d_ref, kk_ref, b_ref, k_ref, r_ref, v_ref, s_out_ref, y_ref):
    S = s_ref[0]
    s_kk = jnp.sum(S * kk_ref[0], axis=-1, keepdims=True)
    S = S * d_ref[0] - s_kk * b_ref[0] + v_ref[0] * k_ref[0]
    s_out_ref[0] = S
    y_ref[0] = jnp.sum(S * r_ref[0], axis=-1)


def _rwkv_step(S0, decay, kk, b, k, r, v):
    B, Hh, N, _ = S0.shape
    row = lambda t: t.reshape(B, Hh, 1, N)
    col = lambda t: t.reshape(B, Hh, N, 1)
    row_spec = pl.BlockSpec((1, Hh, 1, N), lambda i: (i, 0, 0, 0))
    col_spec = pl.BlockSpec((1, Hh, N, 1), lambda i: (i, 0, 0, 0))
    st_spec = pl.BlockSpec((1, Hh, N, N), lambda i: (i, 0, 0, 0))
    S1, y = pl.pallas_call(
        _rwkv_step_body,
        grid=(B,),
        in_specs=[st_spec] + [row_spec] * 5 + [col_spec],
        out_specs=[st_spec, pl.BlockSpec((1, Hh, N), lambda i: (i, 0, 0))],
        out_shape=[jax.ShapeDtypeStruct(S0.shape, F32), jax.ShapeDtypeStruct((B, Hh, N), F32)],
        compiler_params=_cparams(("parallel",)),
        name="rwkv_step",
    )(S0, row(decay), row(kk), row(b), row(k), row(r), col(v))
    return S1, y.reshape(B, Hh * N)


def _swa_prompt_body(q_ref, kp_ref, kc_ref, vp_ref, vc_ref, qg_ref, sink_ref, o_ref):
    i = pl.program_id(1)
    W = WINDOW
    lane = lax.broadcasted_iota(jnp.int32, (1, LANES), 1)
    m0 = lane < HEAD_DIM
    a_idx = lax.broadcasted_iota(jnp.int32, (W, 2 * W), 0)
    c_idx = lax.broadcasted_iota(jnp.int32, (W, 2 * W), 1)
    first_key = jnp.where(i > 0, 0, W)
    mask = (c_idx > a_idx) & (c_idx <= a_idx + W) & (c_idx >= first_key)
    mask_all = jnp.concatenate([mask] * GQA_GROUP, axis=0)
    neg = jnp.float32(-jnp.inf)
    qg = qg_ref[...]

    for blk in range(N_KV_HEADS // 2):
        ksl = slice(blk * LANES, (blk + 1) * LANES)
        kb = jnp.concatenate([kp_ref[0, :, ksl], kc_ref[0, :, ksl]], axis=0)
        vb = jnp.concatenate([vp_ref[0, :, ksl], vc_ref[0, :, ksl]], axis=0)
        kb_r = pltpu.roll(kb, HEAD_DIM, 1)
        vb_r = pltpu.roll(vb, HEAD_DIM, 1)
        for sub in range(2):
            if sub == 0:
                k2 = jnp.where(m0, kb, kb_r)
                v2 = jnp.where(m0, vb, vb_r)
            else:
                k2 = jnp.where(m0, kb_r, kb)
                v2 = jnp.where(m0, vb_r, vb)
            k2 = k2.astype(BF16)
            v2 = v2.astype(BF16)
            kv = 2 * blk + sub
            n_pairs = GQA_GROUP // 2
            q_rows, sink_rows = [], []
            for pp in range(n_pairs):
                pair = kv * n_pairs + pp
                qsl = slice(pair * LANES, (pair + 1) * LANES)
                qv = q_ref[0, :, qsl]
                ms = _head_sum(qv * qv, m0) * (1.0 / HEAD_DIM)
                qn = qv * lax.rsqrt(ms + RMS_EPS) * qg * ATTN_SCALE
                q0 = jnp.where(m0, qn, 0.0)
                q_rows += [q0, qn - q0]
                sv = sink_ref[:, qsl]
                for hm in (m0, jnp.logical_not(m0)):
                    sink = jnp.max(jnp.where(hm, sv, neg), axis=-1, keepdims=True)
                    sink_rows.append(jnp.broadcast_to(sink, (W, LANES)))
            q_all = jnp.concatenate(q_rows, axis=0).astype(BF16)
            sink = jnp.concatenate(sink_rows, axis=0)
            s = lax.dot_general(q_all, k2, (((1,), (1,)), ((), ())), preferred_element_type=F32)
            s = jnp.where(mask_all, s, neg)
            s_a, s_b = s[:, :LANES], s[:, LANES:]
            row_max = jnp.max(jnp.maximum(s_a, s_b), axis=-1, keepdims=True)
            m = jnp.maximum(row_max, sink)
            p_a = jnp.exp(s_a - m)
            p_b = jnp.exp(s_b - m)
            row_sum = jnp.sum(p_a + p_b, axis=-1, keepdims=True)
            inv = 1.0 / (row_sum + jnp.exp(sink - m))
            p = jnp.concatenate([p_a * inv, p_b * inv], axis=1)
            o = jnp.dot(p.astype(BF16), v2, preferred_element_type=F32)
            for pp in range(n_pairs):
                pair = kv * n_pairs + pp
                o0 = o[(2 * pp) * W:(2 * pp + 1) * W]
                o1 = o[(2 * pp + 1) * W:(2 * pp + 2) * W]
                o_ref[0, :, pair * LANES:(pair + 1) * LANES] = jnp.where(m0, o0, o1).astype(o_ref.dtype)


def _swa_prompt(q, k, v, q_norm_g, sinks):
    B, T, D = q.shape
    KVW = k.shape[-1]
    W = WINDOW
    qg = jnp.tile(q_norm_g.astype(F32), 2).reshape(1, LANES)
    sink_l = jnp.repeat(sinks.astype(F32), HEAD_DIM).reshape(1, D)
    cur = pl.BlockSpec((1, W, KVW), lambda b, i: (b, i, 0))
    prev = pl.BlockSpec((1, W, KVW), lambda b, i: (b, jnp.maximum(i - 1, 0), 0))
    return pl.pallas_call(
        _swa_prompt_body,
        grid=(B, T // W),
        in_specs=[pl.BlockSpec((1, W, D), lambda b, i: (b, i, 0)), prev, cur, prev, cur,
                  pl.BlockSpec((1, LANES), lambda b, i: (0, 0)),
                  pl.BlockSpec((1, D), lambda b, i: (0, 0))],
        out_specs=pl.BlockSpec((1, W, D), lambda b, i: (b, i, 0)),
        out_shape=jax.ShapeDtypeStruct((B, T, D), BF16),
        compiler_params=_cparams(("parallel", "parallel")),
        name="swa_prompt",
    )(q, k, k, v, v, qg, sink_l)


def _swa_sample_body(q_ref, kc_ref, vc_ref, kn_ref, vn_ref, qg_ref, sink_ref, o_ref):
    G, HD = GQA_GROUP, HEAD_DIM
    q = q_ref[0]
    ms = jnp.mean(q * q, axis=-1, keepdims=True)
    qn = q * lax.rsqrt(ms + RMS_EPS) * qg_ref[...] * ATTN_SCALE
    n_buf = kc_ref.shape[1]
    col = lax.broadcasted_iota(jnp.int32, (G, n_buf), 1)
    valid = (n_buf - col) < WINDOW
    neg = jnp.float32(-jnp.inf)
    for kv in range(N_KV_HEADS):
        qj = qn[kv * G:(kv + 1) * G, :]
        lsl = slice(kv * HD, (kv + 1) * HD)
        kc = kc_ref[0, :, lsl]
        vc = vc_ref[0, :, lsl]
        kn = kn_ref[0, :, lsl]
        vn = vn_ref[0, :, lsl]
        s_c = jnp.where(valid, _dot_nt(qj, kc), neg)
        s_n = jnp.sum(qj.astype(BF16).astype(F32) * kn.astype(BF16).astype(F32),
                      axis=-1, keepdims=True)
        sink = sink_ref[kv * G:(kv + 1) * G, :]
        m = jnp.maximum(jnp.maximum(jnp.max(s_c, axis=-1, keepdims=True), s_n), sink)
        p_c = jnp.exp(s_c - m)
        p_n = jnp.exp(s_n - m)
        denom = jnp.sum(p_c, axis=-1, keepdims=True) + p_n + jnp.exp(sink - m)
        p_c = p_c / denom
        p_n = p_n / denom
        o = _dot(p_c, vc) + p_n.astype(BF16).astype(F32) * vn.astype(BF16).astype(F32)
        o_ref[0, kv * G:(kv + 1) * G, :] = o.astype(o_ref.dtype)


def _swa_sample(q, k_buf, v_buf, k_new, v_new, q_norm_g, sinks):
    B, D = q.shape
    n_buf, KVW = k_buf.shape[1:]
    NH, HD = N_Q_HEADS, HEAD_DIM
    buf = pl.BlockSpec((1, n_buf, KVW), lambda b: (b, 0, 0))
    new = pl.BlockSpec((1, 1, KVW), lambda b: (b, 0, 0))
    o = pl.pallas_call(
        _swa_sample_body,
        grid=(B,),
        in_specs=[pl.BlockSpec((1, NH, HD), lambda b: (b, 0, 0)), buf, buf, new, new,
                  pl.BlockSpec((1, HD), lambda b: (0, 0)),
                  pl.BlockSpec((NH, 1), lambda b: (0, 0))],
        out_specs=pl.BlockSpec((1, NH, HD), lambda b: (b, 0, 0)),
        out_shape=jax.ShapeDtypeStruct((B, NH, HD), BF16),
        compiler_params=_cparams(("parallel",)),
        name="swa_sample",
    )(q.reshape(B, NH, HD), k_buf, v_buf, k_new, v_new,
      q_norm_g.astype(F32).reshape(1, HD), sinks.astype(F32).reshape(NH, 1))
    return o.reshape(B, D)


def _rms(x, g):
    return x * lax.rsqrt(jnp.mean(x * x, axis=-1, keepdims=True) + RMS_EPS) * g


def _silu(x):
    return x * jax.nn.sigmoid(x)


def _forward(x, mods, kvmod, wkv0, shift0, conv0, k_buf, v_buf, p, wsrc, tag):
    B, T, D = x.shape
    M = B * T
    depth = len(mods)
    n_a = depth // 2
    emit = T == 1
    wb = {}

    def mm(xin, wname, l=None, **kw):
        if emit:
            out, w16 = _mm(xin, wsrc[wname], layer=l, emit_w=True, name=f"{tag}_{wname}", **kw)
            wb[(wname, l)] = w16
            return out
        w = wsrc[wname] if l is None else wsrc[wname][l]
        return _mm(xin, w, name=f"{tag}_{wname}", **kw)

    new_wkv, new_shift, new_conv = [], [], []
    k_att = v_att = k_win = v_win = None
    x2 = x.reshape(M, D)
    for l in range(depth):
        sh1, sc1, gt1, sh2, sc2, gt2 = jnp.split(mods[l], 6, axis=-1)
        x3 = x2.reshape(B, T, D)
        if l < n_a:
            mixes, h_last = _norm_shift(x3, p['ln1_g'][l], sc1, sh1, shift0[l], p['rwkv_mix'][l])
            xr, xw, xk, xv, xa, xg = [m.reshape(M, D) for m in mixes]
            r = mm(xr, 'rwkv_w_r', l)
            k = mm(xk, 'rwkv_w_k', l)
            v = mm(xv, 'rwkv_w_v', l)
            wl = mm(mm(xw, 'rwkv_w1', l, act="tanh", out_dtype=BF16), 'rwkv_w2', l)
            al = mm(mm(xa, 'rwkv_a1', l, out_dtype=BF16), 'rwkv_a2', l)
            g = mm(mm(xg, 'rwkv_g1', l, act="sigmoid", out_dtype=BF16), 'rwkv_g2', l)
            w0, a0 = p['rwkv_w0'][l], p['rwkv_a0'][l]
            k_k, k_a = p['rwkv_k_k'][l], p['rwkv_k_a'][l]
            r_k = p['rwkv_r_k'][l].reshape(D)
            lnx_w, lnx_b = p['rwkv_lnx_w'][l], p['rwkv_lnx_b'][l]
            if T > 1:
                sh = (B, T, D)
                yg, S = _rwkv_prompt(r.reshape(sh), k.reshape(sh), v.reshape(sh), wl.reshape(sh),
                                     al.reshape(sh), g.reshape(sh), w0, a0, k_k, k_a, r_k,
                                     lnx_w, lnx_b)
                yg = yg.reshape(M, D)
            else:
                Hh, N = RW_HEADS, RW_HEAD
                w_log = -jax.nn.softplus(-(w0 + wl)) - 0.5
                decay = jnp.exp(-jnp.exp(w_log))
                a = jax.nn.sigmoid(a0 + al)
                kk = (k * k_k).reshape(B, Hh, N)
                kk = kk / jnp.maximum(jnp.sqrt(jnp.sum(kk * kk, axis=-1, keepdims=True)), 1e-12)
                kk = kk.reshape(B, D)
                k2 = k * (1.0 + (a - 1.0) * k_a)
                S, y = _rwkv_step(wkv0[l].astype(F32), decay, kk, kk * a, k2, r, v)
                y4 = y.reshape(B, Hh, N)
                mu = jnp.mean(y4, axis=-1, keepdims=True)
                var = jnp.mean(jnp.square(y4 - mu), axis=-1, keepdims=True)
                y4 = ((y4 - mu) * lax.rsqrt(var + GN_EPS) * lnx_w.reshape(Hh, N)
                      + lnx_b.reshape(Hh, N))
                bonus = jnp.sum((r * k2 * r_k).reshape(B, Hh, N), axis=-1, keepdims=True)
                y4 = y4 + bonus * v.reshape(B, Hh, N)
                yg = (y4.reshape(B, D) * g).astype(BF16)
            new_wkv.append(S)
            new_shift.append(h_last)
            x2 = mm(yg, 'rwkv_w_o', l, res=x2, gate=gt1, rows_per_gate=T)
        else:
            j = l - n_a
            h = _norm_mod(x3, p['ln1_g'][l], sc1, sh1).reshape(M, D)
            q = mm(h, 'attn_w_q', j)
            if k_buf is None:
                attn = _swa_prompt(q.reshape(B, T, D), k_att, v_att, p['attn_q_norm_g'][j],
                                   p['attn_sinks'][j]).reshape(M, D)
            else:
                attn = _swa_sample(q, k_buf, v_buf, k_att, v_att, p['attn_q_norm_g'][j],
                                   p['attn_sinks'][j])
            x2 = mm(attn, 'attn_w_o', j, res=x2, gate=gt1, rows_per_gate=T)
        h2 = _norm_mod(x2.reshape(B, T, D), p['ln2_g'][l], sc2, sh2).reshape(M, D)
        if emit:
            act, cb, wg16, wu16 = _ffn_in_tok(h2, wsrc['ffn_w_in'], l, p['ffn_conv_w'][l],
                                              p['ffn_conv_b'][l], conv0[l])
            wb[('ffn_w_gate', l)] = wg16
            wb[('ffn_w_up', l)] = wu16
        else:
            act, cb = _ffn_in_seq(h2, wsrc['ffn_w_gate'][l], wsrc['ffn_w_up'][l],
                                  p['ffn_conv_w'][l], p['ffn_conv_b'][l], conv0[l], T)
        new_conv.append(cb)
        x2 = mm(act, 'ffn_w_out', l, res=x2, gate=gt2, rows_per_gate=T)
        if l == n_a - 1:
            sh, sc = jnp.split(kvmod, 2, axis=-1)
            hn = _norm_mod(x2.reshape(B, T, D), p['kv_norm_g'], sc, sh).reshape(M, D)
            kv = mm(hn, 'w_kv')
            KVW = kv.shape[-1] // 2
            k_new = _rms(kv[:, :KVW].reshape(B, T, N_KV_HEADS, HEAD_DIM), p['k_norm_g'])
            v_new = kv[:, KVW:].reshape(B, T, N_KV_HEADS, HEAD_DIM)
            k_att, v_att = k_new.reshape(B, T, KVW), v_new.reshape(B, T, KVW)
            if k_buf is None:
                w = min(WINDOW, T)
                k_win, v_win = k_new[:, T - w:], v_new[:, T - w:]
            else:
                n_buf = k_buf.shape[1]
                k_win = jnp.concatenate([k_buf, k_new], axis=1)[:, -n_buf:]
                v_win = jnp.concatenate([v_buf, v_new], axis=1)[:, -n_buf:]
                k_buf = k_buf.reshape(B, n_buf, KVW)
                v_buf = v_buf.reshape(B, n_buf, KVW)
    outs = (x2.reshape(B, T, D), jnp.stack(new_wkv), jnp.stack(new_shift), jnp.stack(new_conv),
            k_win, v_win)
    return outs, wb


class _Layered:
    def __init__(self, wb):
        self._wb = wb

    def __getitem__(self, name):
        if (name, None) in self._wb:
            return self._wb[(name, None)]
        n = 1 + max(l for (nm, l) in self._wb if nm == name)
        return [self._wb[(name, l)] for l in range(n)]


def kernel(x_prompt, x_sample, c_prompt, c_sample, state_wkv, state_shift, state_conv, cache_k_win, cache_v_win, mod_w, mod_b, ln1_g, ln2_g, rwkv_mix, rwkv_w0, rwkv_w1, rwkv_w2, rwkv_a0, rwkv_a1, rwkv_a2, rwkv_g1, rwkv_g2, rwkv_k_k, rwkv_k_a, rwkv_r_k, rwkv_w_r, rwkv_w_k, rwkv_w_v, rwkv_w_o, rwkv_lnx_w, rwkv_lnx_b, kv_norm_g, kv_mod_w, kv_mod_b, w_kv, k_norm_g, attn_w_q, attn_q_norm_g, attn_sinks, attn_w_o, ffn_w_in, ffn_conv_w, ffn_conv_b, ffn_w_out):
    p = dict(mod_w=mod_w, mod_b=mod_b, ln1_g=ln1_g, ln2_g=ln2_g,
             rwkv_mix=rwkv_mix, rwkv_w0=rwkv_w0, rwkv_w1=rwkv_w1, rwkv_w2=rwkv_w2,
             rwkv_a0=rwkv_a0, rwkv_a1=rwkv_a1, rwkv_a2=rwkv_a2, rwkv_g1=rwkv_g1, rwkv_g2=rwkv_g2,
             rwkv_k_k=rwkv_k_k, rwkv_k_a=rwkv_k_a, rwkv_r_k=rwkv_r_k, rwkv_w_r=rwkv_w_r,
             rwkv_w_k=rwkv_w_k, rwkv_w_v=rwkv_w_v, rwkv_w_o=rwkv_w_o,
             rwkv_lnx_w=rwkv_lnx_w, rwkv_lnx_b=rwkv_lnx_b,
             kv_norm_g=kv_norm_g, kv_mod_w=kv_mod_w, kv_mod_b=kv_mod_b, w_kv=w_kv, k_norm_g=k_norm_g,
             attn_w_q=attn_w_q, attn_q_norm_g=attn_q_norm_g, attn_sinks=attn_sinks, attn_w_o=attn_w_o,
             ffn_w_in=ffn_w_in, ffn_conv_w=ffn_conv_w, ffn_conv_b=ffn_conv_b, ffn_w_out=ffn_w_out)
    depth = mod_w.shape[0]
    n_a = depth // 2
    Bp = x_prompt.shape[0]
    dt = x_prompt.dtype

    c_all = _silu(jnp.concatenate([c_prompt, c_sample], axis=0)).astype(BF16)
    mods = [_mm(c_all, mod_w, layer=l, bias=mod_b[l], name=f"mod{l}") for l in range(depth)]
    kvmod = _mm(c_all, kv_mod_w, bias=kv_mod_b, name="kvmod")

    outs_s, wb = _forward(x_sample, [m[Bp:] for m in mods], kvmod[Bp:], state_wkv, state_shift,
                          state_conv, cache_k_win, cache_v_win, p, p, "s")
    F = ffn_conv_b.shape[-1]
    wkv0 = jnp.zeros((n_a, Bp, RW_HEADS, RW_HEAD, RW_HEAD), dt)
    shift0 = jnp.zeros((n_a, Bp, D_MODEL), dt)
    conv0 = jnp.zeros((depth, Bp, CONV_W - 1, F), dt)
    outs_p, _ = _forward(x_prompt, [m[:Bp] for m in mods], kvmod[:Bp], wkv0, shift0, conv0,
                         None, None, p, _Layered(wb), "p")
    y_p, wkv_p, shift_p, conv_p, kwin_p, vwin_p = outs_p
    y_s, wkv_s, shift_s, conv_s, kwin_s, vwin_s = outs_s
    return (y_p, y_s, wkv_p, wkv_s, shift_p, shift_s, conv_p, conv_s,
            kwin_p, kwin_s, vwin_p, vwin_s)
```

```python
import functools

import jax
import jax.numpy as jnp
from jax import lax
from jax.experimental import pallas as pl
from jax.experimental.pallas import tpu as pltpu

F32 = jnp.float32
BF16 = jnp.bfloat16

D_MODEL = 4096
RW_HEAD = 64
RW_HEADS = D_MODEL // RW_HEAD
GN_EPS = 64e-5
DECAY_SCALE = 0.6065306597126334
RMS_EPS = 1e-6
HEAD_DIM = 64
N_Q_HEADS = D_MODEL // HEAD_DIM
N_KV_HEADS = N_Q_HEADS // 8
GQA_GROUP = N_Q_HEADS // N_KV_HEADS
WINDOW = 128
ATTN_SCALE = HEAD_DIM ** -0.5
CONV_W = 3

LANES = 128
SUBLANES = 8
MXU_COLS = 256
VMEM_LIMIT_BYTES = 56 * 1024 * 1024

RW_CHUNK = 64
RW_LANES_PER_STEP = 2048
RW_SUBCHUNKS = 4
NORM_ROWS = 256


def _cparams(sem):
    return pltpu.CompilerParams(dimension_semantics=sem, vmem_limit_bytes=VMEM_LIMIT_BYTES)


def _pick_tile(dim, prefs):
    for p in prefs:
        if dim >= p and dim % p == 0:
            return p
    return dim


def _sigmoid(x):
    return 1.0 / (1.0 + jnp.exp(-x))


def _mm_body(*refs, nk, has_bias, act, has_res, emit_w):
    x_ref, w_ref = refs[0], refs[1]
    pos = 2
    bias_ref = res_ref = gate_ref = wout_ref = None
    if has_bias:
        bias_ref = refs[pos]; pos += 1
    if has_res:
        res_ref, gate_ref = refs[pos], refs[pos + 1]; pos += 2
    o_ref = refs[pos]; pos += 1
    if emit_w:
        wout_ref = refs[pos]; pos += 1
    acc_ref = refs[pos] if nk > 1 else None

    w = w_ref[...].astype(BF16)
    if emit_w:
        wout_ref[...] = w
    part = jnp.dot(x_ref[...].astype(BF16), w, preferred_element_type=F32)

    def finish(acc):
        if has_bias:
            acc = acc + bias_ref[...]
        if act == "tanh":
            acc = jnp.tanh(acc)
        elif act == "sigmoid":
            acc = _sigmoid(acc)
        if has_res:
            acc = res_ref[...] + gate_ref[...] * acc
        o_ref[...] = acc.astype(o_ref.dtype)

    if nk == 1:
        finish(part)
    else:
        k = pl.program_id(2)

        @pl.when(k == 0)
        def _():
            acc_ref[...] = part

        @pl.when(k > 0)
        def _():
            acc_ref[...] += part

        @pl.when(k == nk - 1)
        def _():
            finish(acc_ref[...])


def _mm(x, w, *, layer=None, bias=None, act=None, res=None, gate=None, rows_per_gate=1,
        out_dtype=F32, emit_w=False, name="mm"):
    M, K = x.shape
    K2, N = w.shape[-2:]
    assert K == K2 and (w.ndim == 2) == (layer is None)
    tm = _pick_tile(M, (1024,))
    tn = _pick_tile(N, (1024, 512))
    tk = _pick_tile(K, (2048,)) if M < 1024 or K <= 2048 else (K if K <= 4096 else
                                                                _pick_tile(K, (3584, 2048)))
    nk = K // tk
    grid = (M // tm, N // tn, nk)
    assert not emit_w or grid[0] == 1

    if layer is None:
        w_spec = pl.BlockSpec((tk, tn), lambda i, j, k: (k, j))
    else:
        w_spec = pl.BlockSpec((None, tk, tn), lambda i, j, k: (layer, k, j))
    in_specs = [pl.BlockSpec((tm, tk), lambda i, j, k: (i, k)), w_spec]
    args = [x, w]
    if bias is not None:
        in_specs.append(pl.BlockSpec((1, tn), lambda i, j, k: (0, j)))
        args.append(bias.reshape(1, N).astype(F32))
    if res is not None:
        in_specs.append(pl.BlockSpec((tm, tn), lambda i, j, k: (i, j)))
        args.append(res)
        if rows_per_gate == 1:
            in_specs.append(pl.BlockSpec((tm, tn), lambda i, j, k: (i, j)))
            args.append(gate)
        else:
            assert rows_per_gate % tm == 0
            bpg = rows_per_gate // tm
            in_specs.append(pl.BlockSpec((None, 1, tn), lambda i, j, k: (i // bpg, 0, j)))
            args.append(gate.reshape(gate.shape[0], 1, N))
    out_specs = [pl.BlockSpec((tm, tn), lambda i, j, k: (i, j))]
    out_shape = [jax.ShapeDtypeStruct((M, N), out_dtype)]
    if emit_w:
        out_specs.append(pl.BlockSpec((tk, tn), lambda i, j, k: (k, j)))
        out_shape.append(jax.ShapeDtypeStruct((K, N), BF16))
    scratch = [pltpu.VMEM((tm, tn), F32)] if nk > 1 else []
    body = functools.partial(_mm_body, nk=nk, has_bias=bias is not None, act=act,
                             has_res=res is not None, emit_w=emit_w)
    outs = pl.pallas_call(
        body,
        grid=grid,
        in_specs=in_specs,
        out_specs=out_specs,
        out_shape=out_shape,
        scratch_shapes=scratch,
        compiler_params=_cparams(("parallel", "parallel", "arbitrary")),
        name=name,
    )(*args)
    return tuple(outs) if emit_w else outs[0]


def _accumulate(k, nk, pairs):
    if nk == 1:
        return

    @pl.when(k == 0)
    def _():
        for acc_ref, part in pairs:
            acc_ref[...] = part

    @pl.when(k > 0)
    def _():
        for acc_ref, part in pairs:
            acc_ref[...] += part


def _ffn_in_seq_body(x_ref, wg_ref, wu_ref, cw_ref, cb_ref, st_ref,
                     act_ref, st_out_ref, carry, *, tiles_per_seq, n_sub):
    i = pl.program_id(1)

    @pl.when(i % tiles_per_seq == 0)
    def _():
        carry[...] = st_ref[...]

    x = x_ref[...]
    tm = x.shape[0]
    ts = act_ref.shape[1] // n_sub
    row = lax.broadcasted_iota(jnp.int32, (SUBLANES, ts), 0)
    for s in range(n_sub):
        sl = slice(s * ts, (s + 1) * ts)
        g = jnp.dot(x, wg_ref[:, sl], preferred_element_type=F32)
        u = jnp.dot(x, wu_ref[:, sl], preferred_element_type=F32)
        tail = carry[:, sl]
        before2, before1 = tail[0:1], tail[1:2]
        r1 = pltpu.roll(g, 1, 0)
        r2 = pltpu.roll(g, 2, 0)
        head1 = jnp.where(row == 0, before1, r1[:SUBLANES])
        head2 = jnp.where(row == 0, before2, jnp.where(row == 1, before1, r2[:SUBLANES]))
        prev1 = jnp.concatenate([head1, r1[SUBLANES:]], axis=0)
        prev2 = jnp.concatenate([head2, r2[SUBLANES:]], axis=0)
        cw = cw_ref[:, sl]
        conv = cb_ref[:, sl] + cw[0:1] * prev2 + cw[1:2] * prev1 + cw[2:3] * g
        act_ref[:, sl] = (conv * _sigmoid(conv) * u).astype(act_ref.dtype)
        last = g[tm - (CONV_W - 1):, :]
        st_out_ref[:, sl] = last
        carry[:, sl] = last


def _ffn_in_seq(x, wg, wu, conv_w, conv_b, state, T):
    M, K = x.shape
    F = wg.shape[1]
    tm = _pick_tile(T, (1024,))
    tn = _pick_tile(F, (512,))
    tps = T // tm
    body = functools.partial(_ffn_in_seq_body, tiles_per_seq=tps, n_sub=tn // MXU_COLS)
    w_spec = pl.BlockSpec((K, tn), lambda j, i: (0, j))
    act, st = pl.pallas_call(
        body,
        grid=(F // tn, M // tm),
        in_specs=[pl.BlockSpec((tm, K), lambda j, i: (i, 0)),
                  w_spec, w_spec,
                  pl.BlockSpec((CONV_W, tn), lambda j, i: (0, j)),
                  pl.BlockSpec((1, tn), lambda j, i: (0, j)),
                  pl.BlockSpec((None, CONV_W - 1, tn), lambda j, i: (i // tps, 0, j))],
        out_specs=[pl.BlockSpec((tm, tn), lambda j, i: (i, j)),
                   pl.BlockSpec((None, CONV_W - 1, tn), lambda j, i: (i, 0, j))],
        out_shape=[jax.ShapeDtypeStruct((M, F), BF16),
                   jax.ShapeDtypeStruct((M // tm, CONV_W - 1, F), F32)],
        scratch_shapes=[pltpu.VMEM((CONV_W - 1, tn), F32)],
        compiler_params=_cparams(("parallel", "arbitrary")),
        name="ffn_in_seq",
    )(x, wg, wu, conv_w.astype(F32), conv_b.reshape(1, F).astype(F32), state.astype(F32))
    return act, st[tps - 1::tps]


def _ffn_in_tok_body(x_ref, wg_ref, wu_ref, cw_ref, cb_ref, s0_ref, s1_ref,
                     act_ref, g_ref, wgo_ref, wuo_ref, accg, accu, *, nk):
    k = pl.program_id(2)
    x = x_ref[...]
    wg = wg_ref[...].astype(BF16)
    wu = wu_ref[...].astype(BF16)
    wgo_ref[...] = wg
    wuo_ref[...] = wu
    pg = jnp.dot(x, wg, preferred_element_type=F32)
    pu = jnp.dot(x, wu, preferred_element_type=F32)
    _accumulate(k, nk, [(accg, pg), (accu, pu)])

    @pl.when(k == nk - 1)
    def _():
        g = accg[...] if nk > 1 else pg
        u = accu[...] if nk > 1 else pu
        cw = cw_ref[...]
        conv = cb_ref[...] + cw[0:1] * s0_ref[...] + cw[1:2] * s1_ref[...] + cw[2:3] * g
        act_ref[...] = (conv * _sigmoid(conv) * u).astype(act_ref.dtype)
        g_ref[...] = g


def _ffn_in_tok(x, w_in, layer, conv_w, conv_b, state):
    B, K = x.shape
    F = w_in.shape[-1] // 2
    tn = _pick_tile(F, (1024, 512))
    tk = _pick_tile(K, (2048,))
    nk = K // tk
    nj = F // tn
    body = functools.partial(_ffn_in_tok_body, nk=nk)
    row_spec = pl.BlockSpec((B, tn), lambda i, j, k: (0, j))
    wo_spec = pl.BlockSpec((tk, tn), lambda i, j, k: (k, j))
    s0 = state[:, 0, :].astype(F32)
    s1 = state[:, 1, :].astype(F32)
    act, g, wg, wu = pl.pallas_call(
        body,
        grid=(1, nj, nk),
        in_specs=[pl.BlockSpec((B, tk), lambda i, j, k: (0, k)),
                  pl.BlockSpec((None, tk, tn), lambda i, j, k: (layer, k, j)),
                  pl.BlockSpec((None, tk, tn), lambda i, j, k: (layer, k, j + nj)),
                  pl.BlockSpec((CONV_W, tn), lambda i, j, k: (0, j)),
                  pl.BlockSpec((1, tn), lambda i, j, k: (0, j)),
                  row_spec, row_spec],
        out_specs=[row_spec, row_spec, wo_spec, wo_spec],
        out_shape=[jax.ShapeDtypeStruct((B, F), BF16), jax.ShapeDtypeStruct((B, F), F32),
                   jax.ShapeDtypeStruct((K, F), BF16), jax.ShapeDtypeStruct((K, F), BF16)],
        scratch_shapes=[pltpu.VMEM((B, tn), F32), pltpu.VMEM((B, tn), F32)],
        compiler_params=_cparams(("parallel", "parallel", "arbitrary")),
        name="ffn_in_tok",
    )(x, w_in, w_in, conv_w.astype(F32), conv_b.reshape(1, F).astype(F32), s0, s1)
    return act, jnp.stack([s1, g], axis=1), wg, wu


def _norm_mod_rows(x, g, sc, sh):
    y = x * lax.rsqrt(jnp.mean(x * x, axis=-1, keepdims=True) + RMS_EPS) * g
    return y * (1.0 + sc) + sh


def _norm_body(x_ref, g_ref, sc_ref, sh_ref, o_ref):
    o_ref[0] = _norm_mod_rows(x_ref[0], g_ref[...], sc_ref[0], sh_ref[0]).astype(o_ref.dtype)


def _norm_mod(x, g, sc, sh):
    B, T, D = x.shape
    tr = _pick_tile(T, (2 * NORM_ROWS,))
    tok = pl.BlockSpec((1, tr, D), lambda b, t: (b, t, 0))
    per_b = pl.BlockSpec((1, 1, D), lambda b, t: (b, 0, 0))
    return pl.pallas_call(
        _norm_body,
        grid=(B, T // tr),
        in_specs=[tok, pl.BlockSpec((1, D), lambda b, t: (0, 0)), per_b, per_b],
        out_specs=tok,
        out_shape=jax.ShapeDtypeStruct((B, T, D), BF16),
        compiler_params=_cparams(("parallel", "parallel")),
        name="norm_mod",
    )(x, g.reshape(1, D), sc.reshape(B, 1, D), sh.reshape(B, 1, D))


def _norm_shift_body(x_ref, g_ref, sc_ref, sh_ref, s0_ref, mix_ref, *rest, n_mix):
    outs = rest[:n_mix]
    last_ref = rest[n_mix]
    carry = rest[n_mix + 1]
    t = pl.program_id(1)

    @pl.when(t == 0)
    def _():
        carry[...] = s0_ref[0]

    h = _norm_mod_rows(x_ref[0], g_ref[...], sc_ref[0], sh_ref[0])
    tr = h.shape[0]
    if tr == 1:
        h_prev = carry[...]
    else:
        row = lax.broadcasted_iota(jnp.int32, h.shape, 0)
        h_prev = jnp.where(row == 0, carry[...], pltpu.roll(h, 1, 0))
    dx = h_prev - h
    mix = mix_ref[...]
    for i in range(n_mix):
        outs[i][0] = (h + dx * mix[i:i + 1]).astype(outs[i].dtype)
    last = h[tr - 1:tr, :]
    carry[...] = last
    last_ref[0] = last


def _norm_shift(x, g, sc, sh, shift_prev, mix):
    B, T, D = x.shape
    n_mix = mix.shape[0]
    tr = _pick_tile(T, (NORM_ROWS,))
    tok = pl.BlockSpec((1, tr, D), lambda b, t: (b, t, 0))
    per_b = pl.BlockSpec((1, 1, D), lambda b, t: (b, 0, 0))
    body = functools.partial(_norm_shift_body, n_mix=n_mix)
    outs = pl.pallas_call(
        body,
        grid=(B, T // tr),
        in_specs=[tok, pl.BlockSpec((1, D), lambda b, t: (0, 0)), per_b, per_b, per_b,
                  pl.BlockSpec((n_mix, D), lambda b, t: (0, 0))],
        out_specs=[tok] * n_mix + [per_b],
        out_shape=[jax.ShapeDtypeStruct((B, T, D), BF16)] * n_mix
        + [jax.ShapeDtypeStruct((B, 1, D), F32)],
        scratch_shapes=[pltpu.VMEM((1, D), F32)],
        compiler_params=_cparams(("parallel", "arbitrary")),
        name="norm_shift",
    )(x, g.reshape(1, D), sc.reshape(B, 1, D), sh.reshape(B, 1, D),
      shift_prev.astype(F32).reshape(B, 1, D), mix.astype(F32))
    return outs[:n_mix], outs[n_mix].reshape(B, D)


def _bdot(a, b):
    return lax.dot_general(a.astype(BF16), b.astype(BF16), (((2,), (1,)), ((0,), (0,))),
                           preferred_element_type=F32)


def _bdot_nt(a, b):
    return lax.dot_general(a.astype(BF16), b.astype(BF16), (((2,), (2,)), ((0,), (0,))),
                           preferred_element_type=F32)


def _bdot_tn(a, b):
    return lax.dot_general(a.astype(BF16), b.astype(BF16), (((1,), (1,)), ((0,), (0,))),
                           preferred_element_type=F32)


def _dot(a, b):
    return jnp.dot(a.astype(BF16), b.astype(BF16), preferred_element_type=F32)


def _dot_nt(a, b):
    return lax.dot_general(a.astype(BF16), b.astype(BF16), (((1,), (1,)), ((), ())),
                           preferred_element_type=F32)


def _head_sum(x, m0):
    s0 = jnp.sum(jnp.where(m0, x, 0.0), axis=-1, keepdims=True)
    s1 = jnp.sum(jnp.where(m0, 0.0, x), axis=-1, keepdims=True)
    return jnp.where(m0, s0, s1)


def _split_heads(x, m0):
    x0 = jnp.where(m0, x, 0.0)
    return jnp.concatenate([x0, x - x0], axis=-2)


def _rwkv_prep(r, k, v, wl, al, g, w0, a0, k_k, k_a, r_k):
    C = RW_CHUNK
    H = RW_HEAD
    P = r.shape[0]
    lane = lax.broadcasted_iota(jnp.int32, (1, 1, LANES), 2)
    m0 = lane < H

    log_decay = -DECAY_SCALE * _sigmoid(w0 + wl)
    a = _sigmoid(a0 + al)
    kk = k * k_k
    kk = kk * lax.rsqrt(jnp.maximum(_head_sum(kk * kk, m0), 1e-24))
    k2 = k * (1.0 + (a - 1.0) * k_a)
    b = kk * a

    ti = lax.broadcasted_iota(jnp.int32, (P, C, C), 1)
    si = lax.broadcasted_iota(jnp.int32, (P, C, C), 2)
    tri = jnp.where(ti >= si, 1.0, 0.0).astype(BF16)
    p_hi = log_decay.astype(BF16)
    rem = log_decay - p_hi.astype(F32)
    p_mid = rem.astype(BF16)
    p_lo = (rem - p_mid.astype(F32)).astype(BF16)
    cum3 = _bdot(tri, jnp.concatenate([p_hi, p_mid, p_lo], axis=2))
    cum = cum3[:, :, :LANES] + cum3[:, :, LANES:2 * LANES] + cum3[:, :, 2 * LANES:]

    mid = cum[:, C // 2 - 1:C // 2, :]
    last = cum[:, C - 1:C, :]
    e_fwd = jnp.exp(cum - mid)
    e_inv = jnp.exp(mid - cum)
    e_prev = jnp.exp(cum - log_decay - mid)
    d_mid = jnp.exp(mid)
    d_end_mid = jnp.exp(last - mid)
    d_end = jnp.exp(last)

    kk_t = kk * e_prev
    r_t = r * e_fwd
    b_t = b * e_inv
    k_t = k2 * e_inv
    bonus = _head_sum(r * k2 * r_k, m0) * v
    return kk_t, r_t, b_t, k_t, v, d_mid, d_end_mid, d_end, bonus, g


def _rwkv_chain(prep, lnx_w, lnx_b, S):
    kk_t, r_t, b_t, k_t, v, d_mid, d_end_mid, d_end, bonus, g = prep
    C = RW_CHUNK
    H = RW_HEAD
    lane = lax.broadcasted_iota(jnp.int32, (1, 1, LANES), 2)
    m0 = lane < H
    q = jnp.concatenate([kk_t, r_t], axis=1)
    q0 = jnp.where(m0, q, 0.0)
    g0 = _bdot_nt(q0, jnp.concatenate([b_t, k_t], axis=1))
    g1 = _bdot_nt(q - q0, jnp.concatenate([k_t, b_t], axis=1))

    t_row = lax.broadcasted_iota(jnp.int32, (1, C, LANES), 1)
    s_col = lax.broadcasted_iota(jnp.int32, (1, C, LANES), 2) & (C - 1)
    strict = t_row > s_col
    incl = t_row >= s_col
    g0t, g0b, g1t, g1b = g0[:, :C], g0[:, C:], g1[:, :C], g1[:, C:]
    l_cat = jnp.where(strict, jnp.where(m0, g0t, g1t), 0.0)
    ak_cat = jnp.where(strict, jnp.where(m0, g1t, g0t), 0.0)
    gb0 = jnp.where(incl, g0b, 0.0)
    gb1 = jnp.where(incl, g1b, 0.0)
    grb_cat = jnp.where(m0, gb0, gb1)

    rr = lax.broadcasted_iota(jnp.int32, (1, LANES, LANES), 1)
    cc = lax.broadcasted_iota(jnp.int32, (1, LANES, LANES), 2)
    bd_mask = (rr < H) == (cc < H)

    def block_diag(x_cat):
        return jnp.where(bd_mask, jnp.concatenate([x_cat, x_cat], axis=1), 0.0)

    t_cat = jnp.where(t_row == s_col, 1.0, 0.0) - l_cat
    m_cat = l_cat
    n = 1
    while 2 * n < C:
        m_cat = _bdot(m_cat, block_diag(m_cat))
        t_cat = t_cat + _bdot(t_cat, block_diag(m_cat))
        n *= 2

    v0 = jnp.where(m0, v, 0.0)
    v1 = v - v0
    kk_h = _bdot(t_cat, _split_heads(kk_t, m0))
    av = _bdot(ak_cat, jnp.concatenate([v1, v0], axis=1))
    u0 = -_bdot(t_cat, _split_heads(av, m0))
    r_h = r_t - _bdot(grb_cat, _split_heads(kk_h, m0))
    u00 = jnp.where(m0, u0, 0.0)
    y0 = _bdot(jnp.concatenate([gb0, gb1], axis=2),
               jnp.concatenate([u00, v0, v1, u0 - u00], axis=1))
    b_end = b_t * d_end_mid
    k_end = k_t * d_end_mid
    s_add = jnp.where(bd_mask, _bdot_tn(jnp.concatenate([u0, v], axis=1),
                                        jnp.concatenate([b_end, k_end], axis=1)), 0.0)
    kb = jnp.where(bd_mask, _bdot_tn(kk_h, b_end), 0.0)

    s_mid = S * d_mid
    y = _bdot_nt(r_h, s_mid) + y0
    s_new = S * d_end - _bdot(s_mid, kb) + s_add

    inv_h = 1.0 / H
    mu = _head_sum(y, m0) * inv_h
    yc = y - mu
    var = _head_sum(yc * yc, m0) * inv_h
    yn = yc * lax.rsqrt(var + GN_EPS) * lnx_w + lnx_b
    return (yn + bonus) * g, s_new


def _rwkv_chunk_body(r_ref, k_ref, v_ref, wl_ref, al_ref, g_ref,
                     w0_ref, a0_ref, kk_ref, ka_ref, rk_ref, lw_ref, lb_ref,
                     y_ref, s_out_ref, s_ref, *, n_pairs, n_sub, n_chunks):
    c = pl.program_id(2)

    @pl.when(c == 0)
    def _():
        s_ref[...] = jnp.zeros_like(s_ref)

    C = RW_CHUNK

    def pairs(ref, lead):
        return jnp.stack([ref[lead + (slice(p * LANES, (p + 1) * LANES),)]
                          for p in range(n_pairs)], axis=0)

    par = [pairs(ref, (slice(None),))
           for ref in (w0_ref, a0_ref, kk_ref, ka_ref, rk_ref, lw_ref, lb_ref)]

    def prep(j):
        rows = (0, slice(j * C, (j + 1) * C))
        tok = [pairs(ref, rows) for ref in (r_ref, k_ref, v_ref, wl_ref, al_ref, g_ref)]
        return _rwkv_prep(*tok, *par[:5])

    S = s_ref[...]
    nxt = prep(0)
    for j in range(n_sub):
        cur = nxt
        if j + 1 < n_sub:
            nxt = prep(j + 1)
        yg, S = _rwkv_chain(cur, par[5], par[6], S)
        for p in range(n_pairs):
            y_ref[0, j * C:(j + 1) * C, p * LANES:(p + 1) * LANES] = yg[p].astype(y_ref.dtype)
    s_ref[...] = S

    @pl.when(c == n_chunks - 1)
    def _():
        s_out_ref[0] = S


def _rwkv_prompt(r, k, v, wl, al, g, w0, a0, k_k, k_a, r_k, lnx_w, lnx_b):
    B, T, D = r.shape
    LW = RW_LANES_PER_STEP
    rows = RW_CHUNK * RW_SUBCHUNKS
    n_pairs = LW // LANES
    n_chunks = T // rows
    tok = pl.BlockSpec((1, rows, LW), lambda b, h, c: (b, c, h))
    par = pl.BlockSpec((1, LW), lambda b, h, c: (0, h))
    body = functools.partial(_rwkv_chunk_body, n_pairs=n_pairs, n_sub=RW_SUBCHUNKS,
                             n_chunks=n_chunks)
    y, s_bd = pl.pallas_call(
        body,
        grid=(B, D // LW, n_chunks),
        in_specs=[tok] * 6 + [par] * 7,
        out_specs=[pl.BlockSpec((1, rows, LW), lambda b, h, c: (b, c, h)),
                   pl.BlockSpec((1, n_pairs, LANES, LANES), lambda b, h, c: (b, h, 0, 0))],
        out_shape=[jax.ShapeDtypeStruct((B, T, D), BF16),
                   jax.ShapeDtypeStruct((B, D // LANES, LANES, LANES), F32)],
        scratch_shapes=[pltpu.VMEM((n_pairs, LANES, LANES), F32)],
        compiler_params=_cparams(("parallel", "parallel", "arbitrary")),
        name="rwkv_chunk",
    )(r, k, v, wl, al, g, *[p.reshape(1, D) for p in (w0, a0, k_k, k_a, r_k, lnx_w, lnx_b)])
    H = RW_HEAD
    s = jnp.stack([s_bd[:, :, :H, :H], s_bd[:, :, H:, H:]], axis=2)
    return y, s.reshape(B, D // H, H, H)


def _rwkv_step_body(s_ref, d_ref, kk_ref, b_ref, k_ref, r_ref, v_ref, s_out_ref, y_ref):
    S = s_ref[0]
    s_kk = jnp.sum(S * kk_ref[0], axis=-1, keepdims=True)
    S = S * d_ref[0] - s_kk * b_ref[0] + v_ref[0] * k_ref[0]
    s_out_ref[0] = S
    y_ref[0] = jnp.sum(S * r_ref[0], axis=-1)


def _rwkv_step(S0, decay, kk, b, k, r, v):
    B, Hh, N, _ = S0.shape
    row = lambda t: t.reshape(B, Hh, 1, N)
    row_spec = pl.BlockSpec((1, Hh, 1, N), lambda i: (i, 0, 0, 0))
    st_spec = pl.BlockSpec((1, Hh, N, N), lambda i: (i, 0, 0, 0))
    v_spread = jnp.broadcast_to(v.reshape(B, Hh, N, 1), (B, Hh, N, N))
    S1, y = pl.pallas_call(
        _rwkv_step_body,
        grid=(B,),
        in_specs=[st_spec] + [row_spec] * 5 + [st_spec],
        out_specs=[st_spec, pl.BlockSpec((1, Hh, N), lambda i: (i, 0, 0))],
        out_shape=[jax.ShapeDtypeStruct(S0.shape, F32), jax.ShapeDtypeStruct((B, Hh, N), F32)],
        compiler_params=_cparams(("parallel",)),
        name="rwkv_step",
    )(S0, row(decay), row(kk), row(b), row(k), row(r), v_spread)
    return S1, y.reshape(B, Hh * N)


def _swa_prompt_body(q_ref, kp_ref, kc_ref, vp_ref, vc_ref, qg_ref, sink_ref, o_ref):
    i = pl.program_id(1)
    W = WINDOW
    lane = lax.broadcasted_iota(jnp.int32, (1, LANES), 1)
    m0 = lane < HEAD_DIM
    a_idx = lax.broadcasted_iota(jnp.int32, (W, 2 * W), 0)
    c_idx = lax.broadcasted_iota(jnp.int32, (W, 2 * W), 1)
    first_key = jnp.where(i > 0, 0, W)
    mask = (c_idx > a_idx) & (c_idx <= a_idx + W) & (c_idx >= first_key)
    mask_all = jnp.concatenate([mask] * GQA_GROUP, axis=0)
    neg = jnp.float32(-jnp.inf)
    qg = qg_ref[...]

    for blk in range(N_KV_HEADS // 2):
        ksl = slice(blk * LANES, (blk + 1) * LANES)
        kb = jnp.concatenate([kp_ref[0, :, ksl], kc_ref[0, :, ksl]], axis=0)
        vb = jnp.concatenate([vp_ref[0, :, ksl], vc_ref[0, :, ksl]], axis=0)
        kb_r = pltpu.roll(kb, HEAD_DIM, 1)
        vb_r = pltpu.roll(vb, HEAD_DIM, 1)
        for sub in range(2):
            if sub == 0:
                k2 = jnp.where(m0, kb, kb_r)
                v2 = jnp.where(m0, vb, vb_r)
            else:
                k2 = jnp.where(m0, kb_r, kb)
                v2 = jnp.where(m0, vb_r, vb)
            k2 = k2.astype(BF16)
            v2 = v2.astype(BF16)
            kv = 2 * blk + sub
            n_pairs = GQA_GROUP // 2
            q_rows, sink_rows = [], []
            for pp in range(n_pairs):
                pair = kv * n_pairs + pp
                qsl = slice(pair * LANES, (pair + 1) * LANES)
                qv = q_ref[0, :, qsl]
                ms = _head_sum(qv * qv, m0) * (1.0 / HEAD_DIM)
                qn = qv * lax.rsqrt(ms + RMS_EPS) * qg * ATTN_SCALE
                q0 = jnp.where(m0, qn, 0.0)
                q_rows += [q0, qn - q0]
                sv = sink_ref[:, qsl]
                for hm in (m0, jnp.logical_not(m0)):
                    sink = jnp.max(jnp.where(hm, sv, neg), axis=-1, keepdims=True)
                    sink_rows.append(jnp.broadcast_to(sink, (W, LANES)))
            q_all = jnp.concatenate(q_rows, axis=0).astype(BF16)
            sink = jnp.concatenate(sink_rows, axis=0)
            s = lax.dot_general(q_all, k2, (((1,), (1,)), ((), ())), preferred_element_type=F32)
            s = jnp.where(mask_all, s, neg)
            s_a, s_b = s[:, :LANES], s[:, LANES:]
            row_max = jnp.max(jnp.maximum(s_a, s_b), axis=-1, keepdims=True)
            m = jnp.maximum(row_max, sink)
            p_a = jnp.exp(s_a - m)
            p_b = jnp.exp(s_b - m)
            row_sum = jnp.sum(p_a + p_b, axis=-1, keepdims=True)
            inv = 1.0 / (row_sum + jnp.exp(sink - m))
            p = jnp.concatenate([p_a * inv, p_b * inv], axis=1)
            o = jnp.dot(p.astype(BF16), v2, preferred_element_type=F32)
            for pp in range(n_pairs):
                pair = kv * n_pairs + pp
                o0 = o[(2 * pp) * W:(2 * pp + 1) * W]
                o1 = o[(2 * pp + 1) * W:(2 * pp + 2) * W]
                o_ref[0, :, pair * LANES:(pair + 1) * LANES] = jnp.where(m0, o0, o1).astype(o_ref.dtype)


def _swa_prompt(q, k, v, q_norm_g, sinks):
    B, T, D = q.shape
    KVW = k.shape[-1]
    W = WINDOW
    qg = jnp.tile(q_norm_g.astype(F32), 2).reshape(1, LANES)
    sink_l = jnp.repeat(sinks.astype(F32), HEAD_DIM).reshape(1, D)
    cur = pl.BlockSpec((1, W, KVW), lambda b, i: (b, i, 0))
    prev = pl.BlockSpec((1, W, KVW), lambda b, i: (b, jnp.maximum(i - 1, 0), 0))
    return pl.pallas_call(
        _swa_prompt_body,
        grid=(B, T // W),
        in_specs=[pl.BlockSpec((1, W, D), lambda b, i: (b, i, 0)), prev, cur, prev, cur,
                  pl.BlockSpec((1, LANES), lambda b, i: (0, 0)),
                  pl.BlockSpec((1, D), lambda b, i: (0, 0))],
        out_specs=pl.BlockSpec((1, W, D), lambda b, i: (b, i, 0)),
        out_shape=jax.ShapeDtypeStruct((B, T, D), BF16),
        compiler_params=_cparams(("parallel", "parallel")),
        name="swa_prompt",
    )(q, k, k, v, v, qg, sink_l)


def _swa_sample_body(q_ref, kc_ref, vc_ref, kn_ref, vn_ref, qg_ref, sink_ref, o_ref):
    G, HD = GQA_GROUP, HEAD_DIM
    q = q_ref[0]
    ms = jnp.mean(q * q, axis=-1, keepdims=True)
    qn = q * lax.rsqrt(ms + RMS_EPS) * qg_ref[...] * ATTN_SCALE
    n_buf = kc_ref.shape[1]
    col = lax.broadcasted_iota(jnp.int32, (G, n_buf), 1)
    valid = (n_buf - col) < WINDOW
    neg = jnp.float32(-jnp.inf)
    KV = N_KV_HEADS

    def per_kv(ref):
        return jnp.stack([ref[0, :, kv * HD:(kv + 1) * HD] for kv in range(KV)], axis=0)

    q3 = jnp.stack([qn[kv * G:(kv + 1) * G, :] for kv in range(KV)], axis=0)
    kc, vc, kn, vn = per_kv(kc_ref), per_kv(vc_ref), per_kv(kn_ref), per_kv(vn_ref)
    sink = jnp.stack([sink_ref[kv * G:(kv + 1) * G, :] for kv in range(KV)], axis=0)
    s_c = jnp.where(valid, _bdot_nt(q3, kc), neg)
    s_n = jnp.sum(q3.astype(BF16).astype(F32) * kn.astype(BF16).astype(F32),
                  axis=-1, keepdims=True)
    m = jnp.maximum(jnp.maximum(jnp.max(s_c, axis=-1, keepdims=True), s_n), sink)
    p_c = jnp.exp(s_c - m)
    p_n = jnp.exp(s_n - m)
    denom = jnp.sum(p_c, axis=-1, keepdims=True) + p_n + jnp.exp(sink - m)
    p_c = p_c / denom
    p_n = p_n / denom
    o = _bdot(p_c, vc) + p_n.astype(BF16).astype(F32) * vn.astype(BF16).astype(F32)
    for kv in range(KV):
        o_ref[0, kv * G:(kv + 1) * G, :] = o[kv].astype(o_ref.dtype)


def _swa_sample(q, k_buf, v_buf, k_new, v_new, q_norm_g, sinks):
    B, D = q.shape
    n_buf, KVW = k_buf.shape[1:]
    NH, HD = N_Q_HEADS, HEAD_DIM
    buf = pl.BlockSpec((1, n_buf, KVW), lambda b: (b, 0, 0))
    new = pl.BlockSpec((1, 1, KVW), lambda b: (b, 0, 0))
    o = pl.pallas_call(
        _swa_sample_body,
        grid=(B,),
        in_specs=[pl.BlockSpec((1, NH, HD), lambda b: (b, 0, 0)), buf, buf, new, new,
                  pl.BlockSpec((1, HD), lambda b: (0, 0)),
                  pl.BlockSpec((NH, 1), lambda b: (0, 0))],
        out_specs=pl.BlockSpec((1, NH, HD), lambda b: (b, 0, 0)),
        out_shape=jax.ShapeDtypeStruct((B, NH, HD), BF16),
        compiler_params=_cparams(("parallel",)),
        name="swa_sample",
    )(q.reshape(B, NH, HD), k_buf, v_buf, k_new, v_new,
      q_norm_g.astype(F32).reshape(1, HD), sinks.astype(F32).reshape(NH, 1))
    return o.reshape(B, D)


def _rms(x, g):
    return x * lax.rsqrt(jnp.mean(x * x, axis=-1, keepdims=True) + RMS_EPS) * g


def _silu(x):
    return x * jax.nn.sigmoid(x)


def _forward(x, mods, kvmod, wkv0, shift0, conv0, k_buf, v_buf, p, wsrc, tag):
    B, T, D = x.shape
    M = B * T
    depth = len(mods)
    n_a = depth // 2
    emit = T == 1
    wb = {}

    def mm(xin, wname, l=None, **kw):
        if emit:
            out, w16 = _mm(xin, wsrc[wname], layer=l, emit_w=True, name=f"{tag}_{wname}", **kw)
            wb[(wname, l)] = w16
            return out
        w = wsrc[wname] if l is None else wsrc[wname][l]
        return _mm(xin, w, name=f"{tag}_{wname}", **kw)

    new_wkv, new_shift, new_conv = [], [], []
    k_att = v_att = k_win = v_win = None
    x2 = x.reshape(M, D)
    for l in range(depth):
        sh1, sc1, gt1, sh2, sc2, gt2 = jnp.split(mods[l], 6, axis=-1)
        x3 = x2.reshape(B, T, D)
        if l < n_a:
            mixes, h_last = _norm_shift(x3, p['ln1_g'][l], sc1, sh1, shift0[l], p['rwkv_mix'][l])
            xr, xw, xk, xv, xa, xg = [m.reshape(M, D) for m in mixes]
            r = mm(xr, 'rwkv_w_r', l)
            k = mm(xk, 'rwkv_w_k', l)
            v = mm(xv, 'rwkv_w_v', l)
            wl = mm(mm(xw, 'rwkv_w1', l, act="tanh", out_dtype=BF16), 'rwkv_w2', l)
            al = mm(mm(xa, 'rwkv_a1', l, out_dtype=BF16), 'rwkv_a2', l)
            g = mm(mm(xg, 'rwkv_g1', l, act="sigmoid", out_dtype=BF16), 'rwkv_g2', l)
            w0, a0 = p['rwkv_w0'][l], p['rwkv_a0'][l]
            k_k, k_a = p['rwkv_k_k'][l], p['rwkv_k_a'][l]
            r_k = p['rwkv_r_k'][l].reshape(D)
            lnx_w, lnx_b = p['rwkv_lnx_w'][l], p['rwkv_lnx_b'][l]
            if T > 1:
                sh = (B, T, D)
                yg, S = _rwkv_prompt(r.reshape(sh), k.reshape(sh), v.reshape(sh), wl.reshape(sh),
                                     al.reshape(sh), g.reshape(sh), w0, a0, k_k, k_a, r_k,
                                     lnx_w, lnx_b)
                yg = yg.reshape(M, D)
            else:
                Hh, N = RW_HEADS, RW_HEAD
                w_log = -jax.nn.softplus(-(w0 + wl)) - 0.5
                decay = jnp.exp(-jnp.exp(w_log))
                a = jax.nn.sigmoid(a0 + al)
                kk = (k * k_k).reshape(B, Hh, N)
                kk = kk / jnp.maximum(jnp.sqrt(jnp.sum(kk * kk, axis=-1, keepdims=True)), 1e-12)
                kk = kk.reshape(B, D)
                k2 = k * (1.0 + (a - 1.0) * k_a)
                S, y = _rwkv_step(wkv0[l].astype(F32), decay, kk, kk * a, k2, r, v)
                y4 = y.reshape(B, Hh, N)
                mu = jnp.mean(y4, axis=-1, keepdims=True)
                var = jnp.mean(jnp.square(y4 - mu), axis=-1, keepdims=True)
                y4 = ((y4 - mu) * lax.rsqrt(var + GN_EPS) * lnx_w.reshape(Hh, N)
                      + lnx_b.reshape(Hh, N))
                bonus = jnp.sum((r * k2 * r_k).reshape(B, Hh, N), axis=-1, keepdims=True)
                y4 = y4 + bonus * v.reshape(B, Hh, N)
                yg = (y4.reshape(B, D) * g).astype(BF16)
            new_wkv.append(S)
            new_shift.append(h_last)
            x2 = mm(yg, 'rwkv_w_o', l, res=x2, gate=gt1, rows_per_gate=T)
        else:
            j = l - n_a
            h = _norm_mod(x3, p['ln1_g'][l], sc1, sh1).reshape(M, D)
            q = mm(h, 'attn_w_q', j)
            if k_buf is None:
                attn = _swa_prompt(q.reshape(B, T, D), k_att, v_att, p['attn_q_norm_g'][j],
                                   p['attn_sinks'][j]).reshape(M, D)
            else:
                attn = _swa_sample(q, k_buf, v_buf, k_att, v_att, p['attn_q_norm_g'][j],
                                   p['attn_sinks'][j])
            x2 = mm(attn, 'attn_w_o', j, res=x2, gate=gt1, rows_per_gate=T)
        h2 = _norm_mod(x2.reshape(B, T, D), p['ln2_g'][l], sc2, sh2).reshape(M, D)
        if emit:
            act, cb, wg16, wu16 = _ffn_in_tok(h2, wsrc['ffn_w_in'], l, p['ffn_conv_w'][l],
                                              p['ffn_conv_b'][l], conv0[l])
            wb[('ffn_w_gate', l)] = wg16
            wb[('ffn_w_up', l)] = wu16
        else:
            act, cb = _ffn_in_seq(h2, wsrc['ffn_w_gate'][l], wsrc['ffn_w_up'][l],
                                  p['ffn_conv_w'][l], p['ffn_conv_b'][l], conv0[l], T)
        new_conv.append(cb)
        x2 = mm(act, 'ffn_w_out', l, res=x2, gate=gt2, rows_per_gate=T)
        if l == n_a - 1:
            sh, sc = jnp.split(kvmod, 2, axis=-1)
            hn = _norm_mod(x2.reshape(B, T, D), p['kv_norm_g'], sc, sh).reshape(M, D)
            kv = mm(hn, 'w_kv')
            KVW = kv.shape[-1] // 2
            k_new = _rms(kv[:, :KVW].reshape(B, T, N_KV_HEADS, HEAD_DIM), p['k_norm_g'])
            v_new = kv[:, KVW:].reshape(B, T, N_KV_HEADS, HEAD_DIM)
            k_att, v_att = k_new.reshape(B, T, KVW), v_new.reshape(B, T, KVW)
            if k_buf is None:
                w = min(WINDOW, T)
                k_win, v_win = k_new[:, T - w:], v_new[:, T - w:]
            else:
                n_buf = k_buf.shape[1]
                k_win = jnp.concatenate([k_buf, k_new], axis=1)[:, -n_buf:]
                v_win = jnp.concatenate([v_buf, v_new], axis=1)[:, -n_buf:]
                k_buf = k_buf.reshape(B, n_buf, KVW)
                v_buf = v_buf.reshape(B, n_buf, KVW)
    outs = (x2.reshape(B, T, D), jnp.stack(new_wkv), jnp.stack(new_shift), jnp.stack(new_conv),
            k_win, v_win)
    return outs, wb


class _Layered:
    def __init__(self, wb):
        self._wb = wb

    def __getitem__(self, name):
        if (name, None) in self._wb:
            return self._wb[(name, None)]
        n = 1 + max(l for (nm, l) in self._wb if nm == name)
        return [self._wb[(name, l)] for l in range(n)]


def kernel(x_prompt, x_sample, c_prompt, c_sample, state_wkv, state_shift, state_conv, cache_k_win, cache_v_win, mod_w, mod_b, ln1_g, ln2_g, rwkv_mix, rwkv_w0, rwkv_w1, rwkv_w2, rwkv_a0, rwkv_a1, rwkv_a2, rwkv_g1, rwkv_g2, rwkv_k_k, rwkv_k_a, rwkv_r_k, rwkv_w_r, rwkv_w_k, rwkv_w_v, rwkv_w_o, rwkv_lnx_w, rwkv_lnx_b, kv_norm_g, kv_mod_w, kv_mod_b, w_kv, k_norm_g, attn_w_q, attn_q_norm_g, attn_sinks, attn_w_o, ffn_w_in, ffn_conv_w, ffn_conv_b, ffn_w_out):
    p = dict(mod_w=mod_w, mod_b=mod_b, ln1_g=ln1_g, ln2_g=ln2_g,
             rwkv_mix=rwkv_mix, rwkv_w0=rwkv_w0, rwkv_w1=rwkv_w1, rwkv_w2=rwkv_w2,
             rwkv_a0=rwkv_a0, rwkv_a1=rwkv_a1, rwkv_a2=rwkv_a2, rwkv_g1=rwkv_g1, rwkv_g2=rwkv_g2,
             rwkv_k_k=rwkv_k_k, rwkv_k_a=rwkv_k_a, rwkv_r_k=rwkv_r_k, rwkv_w_r=rwkv_w_r,
             rwkv_w_k=rwkv_w_k, rwkv_w_v=rwkv_w_v, rwkv_w_o=rwkv_w_o,
             rwkv_lnx_w=rwkv_lnx_w, rwkv_lnx_b=rwkv_lnx_b,
             kv_norm_g=kv_norm_g, kv_mod_w=kv_mod_w, kv_mod_b=kv_mod_b, w_kv=w_kv, k_norm_g=k_norm_g,
             attn_w_q=attn_w_q, attn_q_norm_g=attn_q_norm_g, attn_sinks=attn_sinks, attn_w_o=attn_w_o,
             ffn_w_in=ffn_w_in, ffn_conv_w=ffn_conv_w, ffn_conv_b=ffn_conv_b, ffn_w_out=ffn_w_out)
    depth = mod_w.shape[0]
    n_a = depth // 2
    Bp = x_prompt.shape[0]
    dt = x_prompt.dtype

    c_all = _silu(jnp.concatenate([c_prompt, c_sample], axis=0)).astype(BF16)
    mods = [_mm(c_all, mod_w, layer=l, bias=mod_b[l], name=f"mod{l}") for l in range(depth)]
    kvmod = _mm(c_all, kv_mod_w, bias=kv_mod_b, name="kvmod")

    outs_s, wb = _forward(x_sample, [m[Bp:] for m in mods], kvmod[Bp:], state_wkv, state_shift,
                          state_conv, cache_k_win, cache_v_win, p, p, "s")
    F = ffn_conv_b.shape[-1]
    wkv0 = jnp.zeros((n_a, Bp, RW_HEADS, RW_HEAD, RW_HEAD), dt)
    shift0 = jnp.zeros((n_a, Bp, D_MODEL), dt)
    conv0 = jnp.zeros((depth, Bp, CONV_W - 1, F), dt)
    outs_p, _ = _forward(x_prompt, [m[:Bp] for m in mods], kvmod[:Bp], wkv0, shift0, conv0,
                         None, None, p, _Layered(wb), "p")
    y_p, wkv_p, shift_p, conv_p, kwin_p, vwin_p = outs_p
    y_s, wkv_s, shift_s, conv_s, kwin_s, vwin_s = outs_s
    return (y_p, y_s, wkv_p, wkv_s, shift_p, shift_s, conv_p, conv_s,
            kwin_p, kwin_s, vwin_p, vwin_s)
```

```python
import functools

import jax
import jax.numpy as jnp
from jax import lax
from jax.experimental import pallas as pl
from jax.experimental.pallas import tpu as pltpu

F32 = jnp.float32
BF16 = jnp.bfloat16

D_MODEL = 4096
RW_HEAD = 64
RW_HEADS = D_MODEL // RW_HEAD
GN_EPS = 64e-5
DECAY_SCALE = 0.6065306597126334
RMS_EPS = 1e-6
HEAD_DIM = 64
N_Q_HEADS = D_MODEL // HEAD_DIM
N_KV_HEADS = N_Q_HEADS // 8
GQA_GROUP = N_Q_HEADS // N_KV_HEADS
WINDOW = 128
ATTN_SCALE = HEAD_DIM ** -0.5
CONV_W = 3

LANES = 128
SUBLANES = 8
MXU_COLS = 256
VMEM_LIMIT_BYTES = 56 * 1024 * 1024

RW_CHUNK = 64
RW_LANES_PER_STEP = 2048
RW_SUBCHUNKS = 4
NORM_ROWS = 256


def _cparams(sem):
    return pltpu.CompilerParams(dimension_semantics=sem, vmem_limit_bytes=VMEM_LIMIT_BYTES)


def _pick_tile(dim, prefs):
    for p in prefs:
        if dim >= p and dim % p == 0:
            return p
    return dim


def _sigmoid(x):
    return 1.0 / (1.0 + jnp.exp(-x))


def _mm_body(*refs, nk, has_bias, act, has_res, emit_w):
    x_ref, w_ref = refs[0], refs[1]
    pos = 2
    bias_ref = res_ref = gate_ref = wout_ref = None
    if has_bias:
        bias_ref = refs[pos]; pos += 1
    if has_res:
        res_ref, gate_ref = refs[pos], refs[pos + 1]; pos += 2
    o_ref = refs[pos]; pos += 1
    if emit_w:
        wout_ref = refs[pos]; pos += 1
    acc_ref = refs[pos] if nk > 1 else None

    def partial_product():
        w = w_ref[...].astype(BF16)
        if emit_w:
            wout_ref[...] = w
        return jnp.dot(x_ref[...].astype(BF16), w, preferred_element_type=F32)

    def finish(acc):
        if has_bias:
            acc = acc + bias_ref[...]
        if act == "tanh":
            acc = jnp.tanh(acc)
        elif act == "sigmoid":
            acc = _sigmoid(acc)
        if has_res:
            acc = res_ref[...] + gate_ref[...] * acc
        o_ref[...] = acc.astype(o_ref.dtype)

    if nk == 1:
        finish(partial_product())
    else:
        k = pl.program_id(2)

        @pl.when(k == 0)
        def _():
            acc_ref[...] = partial_product()

        @pl.when((k > 0) & (k < nk - 1))
        def _():
            acc_ref[...] = acc_ref[...] + partial_product()

        @pl.when(k == nk - 1)
        def _():
            finish(acc_ref[...] + partial_product())


def _mm(x, w, *, layer=None, bias=None, act=None, res=None, gate=None, rows_per_gate=1,
        out_dtype=F32, emit_w=False, name="mm"):
    M, K = x.shape
    K2, N = w.shape[-2:]
    assert K == K2 and (w.ndim == 2) == (layer is None)
    tm = _pick_tile(M, (1024,))
    tn = _pick_tile(N, (1024, 512))
    tk = _pick_tile(K, (2048,)) if M < 1024 or K <= 2048 else (K if K <= 4096 else
                                                                _pick_tile(K, (3584, 2048)))
    nk = K // tk
    grid = (M // tm, N // tn, nk)
    assert not emit_w or grid[0] == 1

    if layer is None:
        w_spec = pl.BlockSpec((tk, tn), lambda i, j, k: (k, j))
    else:
        w_spec = pl.BlockSpec((None, tk, tn), lambda i, j, k: (layer, k, j))
    in_specs = [pl.BlockSpec((tm, tk), lambda i, j, k: (i, k)), w_spec]
    args = [x, w]
    if bias is not None:
        in_specs.append(pl.BlockSpec((1, tn), lambda i, j, k: (0, j)))
        args.append(bias.reshape(1, N).astype(F32))
    if res is not None:
        in_specs.append(pl.BlockSpec((tm, tn), lambda i, j, k: (i, j)))
        args.append(res)
        if rows_per_gate == 1:
            in_specs.append(pl.BlockSpec((tm, tn), lambda i, j, k: (i, j)))
            args.append(gate)
        else:
            assert rows_per_gate % tm == 0
            bpg = rows_per_gate // tm
            in_specs.append(pl.BlockSpec((None, 1, tn), lambda i, j, k: (i // bpg, 0, j)))
            args.append(gate.reshape(gate.shape[0], 1, N))
    out_specs = [pl.BlockSpec((tm, tn), lambda i, j, k: (i, j))]
    out_shape = [jax.ShapeDtypeStruct((M, N), out_dtype)]
    if emit_w:
        out_specs.append(pl.BlockSpec((tk, tn), lambda i, j, k: (k, j)))
        out_shape.append(jax.ShapeDtypeStruct((K, N), BF16))
    scratch = [pltpu.VMEM((tm, tn), F32)] if nk > 1 else []
    body = functools.partial(_mm_body, nk=nk, has_bias=bias is not None, act=act,
                             has_res=res is not None, emit_w=emit_w)
    outs = pl.pallas_call(
        body,
        grid=grid,
        in_specs=in_specs,
        out_specs=out_specs,
        out_shape=out_shape,
        scratch_shapes=scratch,
        compiler_params=_cparams(("parallel", "parallel", "arbitrary")),
        name=name,
    )(*args)
    return tuple(outs) if emit_w else outs[0]


def _accumulate(k, nk, pairs):
    if nk == 1:
        return

    @pl.when(k == 0)
    def _():
        for acc_ref, part in pairs:
            acc_ref[...] = part

    @pl.when(k > 0)
    def _():
        for acc_ref, part in pairs:
            acc_ref[...] += part


def _ffn_in_seq_body(x_ref, wg_ref, wu_ref, cw_ref, cb_ref, st_ref,
                     act_ref, st_out_ref, carry, *, tiles_per_seq, n_sub):
    i = pl.program_id(1)

    @pl.when(i % tiles_per_seq == 0)
    def _():
        carry[...] = st_ref[...]

    x = x_ref[...]
    tm = x.shape[0]
    ts = act_ref.shape[1] // n_sub
    row = lax.broadcasted_iota(jnp.int32, (SUBLANES, ts), 0)
    for s in range(n_sub):
        sl = slice(s * ts, (s + 1) * ts)
        g = jnp.dot(x, wg_ref[:, sl], preferred_element_type=F32)
        u = jnp.dot(x, wu_ref[:, sl], preferred_element_type=F32)
        tail = carry[:, sl]
        before2, before1 = tail[0:1], tail[1:2]
        r1 = pltpu.roll(g, 1, 0)
        r2 = pltpu.roll(g, 2, 0)
        head1 = jnp.where(row == 0, before1, r1[:SUBLANES])
        head2 = jnp.where(row == 0, before2, jnp.where(row == 1, before1, r2[:SUBLANES]))
        prev1 = jnp.concatenate([head1, r1[SUBLANES:]], axis=0)
        prev2 = jnp.concatenate([head2, r2[SUBLANES:]], axis=0)
        cw = cw_ref[:, sl]
        conv = cb_ref[:, sl] + cw[0:1] * prev2 + cw[1:2] * prev1 + cw[2:3] * g
        act_ref[:, sl] = (conv * _sigmoid(conv) * u).astype(act_ref.dtype)
        last = g[tm - (CONV_W - 1):, :]
        st_out_ref[:, sl] = last
        carry[:, sl] = last


def _ffn_in_seq(x, wg, wu, conv_w, conv_b, state, T):
    M, K = x.shape
    F = wg.shape[1]
    tm = _pick_tile(T, (1024,))
    tn = _pick_tile(F, (512,))
    tps = T // tm
    body = functools.partial(_ffn_in_seq_body, tiles_per_seq=tps, n_sub=tn // MXU_COLS)
    w_spec = pl.BlockSpec((K, tn), lambda j, i: (0, j))
    act, st = pl.pallas_call(
        body,
        grid=(F // tn, M // tm),
        in_specs=[pl.BlockSpec((tm, K), lambda j, i: (i, 0)),
                  w_spec, w_spec,
                  pl.BlockSpec((CONV_W, tn), lambda j, i: (0, j)),
                  pl.BlockSpec((1, tn), lambda j, i: (0, j)),
                  pl.BlockSpec((None, CONV_W - 1, tn), lambda j, i: (i // tps, 0, j))],
        out_specs=[pl.BlockSpec((tm, tn), lambda j, i: (i, j)),
                   pl.BlockSpec((None, CONV_W - 1, tn), lambda j, i: (i, 0, j))],
        out_shape=[jax.ShapeDtypeStruct((M, F), BF16),
                   jax.ShapeDtypeStruct((M // tm, CONV_W - 1, F), F32)],
        scratch_shapes=[pltpu.VMEM((CONV_W - 1, tn), F32)],
        compiler_params=_cparams(("parallel", "arbitrary")),
        name="ffn_in_seq",
    )(x, wg, wu, conv_w.astype(F32), conv_b.reshape(1, F).astype(F32), state.astype(F32))
    return act, st[tps - 1::tps]


def _ffn_in_tok_body(x_ref, wg_ref, wu_ref, cw_ref, cb_ref, s0_ref, s1_ref,
                     act_ref, g_ref, wgo_ref, wuo_ref, accg, accu, *, nk):
    k = pl.program_id(2)
    x = x_ref[...]
    wg = wg_ref[...].astype(BF16)
    wu = wu_ref[...].astype(BF16)
    wgo_ref[...] = wg
    wuo_ref[...] = wu
    pg = jnp.dot(x, wg, preferred_element_type=F32)
    pu = jnp.dot(x, wu, preferred_element_type=F32)
    _accumulate(k, nk, [(accg, pg), (accu, pu)])

    @pl.when(k == nk - 1)
    def _():
        g = accg[...] if nk > 1 else pg
        u = accu[...] if nk > 1 else pu
        cw = cw_ref[...]
        conv = cb_ref[...] + cw[0:1] * s0_ref[...] + cw[1:2] * s1_ref[...] + cw[2:3] * g
        act_ref[...] = (conv * _sigmoid(conv) * u).astype(act_ref.dtype)
        g_ref[...] = g


def _ffn_in_tok(x, w_in, layer, conv_w, conv_b, state):
    B, K = x.shape
    F = w_in.shape[-1] // 2
    tn = _pick_tile(F, (1024, 512))
    tk = _pick_tile(K, (2048,))
    nk = K // tk
    nj = F // tn
    body = functools.partial(_ffn_in_tok_body, nk=nk)
    row_spec = pl.BlockSpec((B, tn), lambda i, j, k: (0, j))
    wo_spec = pl.BlockSpec((tk, tn), lambda i, j, k: (k, j))
    s0 = state[:, 0, :].astype(F32)
    s1 = state[:, 1, :].astype(F32)
    act, g, wg, wu = pl.pallas_call(
        body,
        grid=(1, nj, nk),
        in_specs=[pl.BlockSpec((B, tk), lambda i, j, k: (0, k)),
                  pl.BlockSpec((None, tk, tn), lambda i, j, k: (layer, k, j)),
                  pl.BlockSpec((None, tk, tn), lambda i, j, k: (layer, k, j + nj)),
                  pl.BlockSpec((CONV_W, tn), lambda i, j, k: (0, j)),
                  pl.BlockSpec((1, tn), lambda i, j, k: (0, j)),
                  row_spec, row_spec],
        out_specs=[row_spec, row_spec, wo_spec, wo_spec],
        out_shape=[jax.ShapeDtypeStruct((B, F), BF16), jax.ShapeDtypeStruct((B, F), F32),
                   jax.ShapeDtypeStruct((K, F), BF16), jax.ShapeDtypeStruct((K, F), BF16)],
        scratch_shapes=[pltpu.VMEM((B, tn), F32), pltpu.VMEM((B, tn), F32)],
        compiler_params=_cparams(("parallel", "parallel", "arbitrary")),
        name="ffn_in_tok",
    )(x, w_in, w_in, conv_w.astype(F32), conv_b.reshape(1, F).astype(F32), s0, s1)
    return act, jnp.stack([s1, g], axis=1), wg, wu


def _norm_mod_rows(x, g, sc, sh):
    y = x * lax.rsqrt(jnp.mean(x * x, axis=-1, keepdims=True) + RMS_EPS) * g
    return y * (1.0 + sc) + sh


def _norm_body(x_ref, g_ref, sc_ref, sh_ref, o_ref):
    o_ref[0] = _norm_mod_rows(x_ref[0], g_ref[...], sc_ref[0], sh_ref[0]).astype(o_ref.dtype)


def _norm_mod(x, g, sc, sh):
    B, T, D = x.shape
    tr = _pick_tile(T, (2 * NORM_ROWS,))
    tok = pl.BlockSpec((1, tr, D), lambda b, t: (b, t, 0))
    per_b = pl.BlockSpec((1, 1, D), lambda b, t: (b, 0, 0))
    return pl.pallas_call(
        _norm_body,
        grid=(B, T // tr),
        in_specs=[tok, pl.BlockSpec((1, D), lambda b, t: (0, 0)), per_b, per_b],
        out_specs=tok,
        out_shape=jax.ShapeDtypeStruct((B, T, D), BF16),
        compiler_params=_cparams(("parallel", "parallel")),
        name="norm_mod",
    )(x, g.reshape(1, D), sc.reshape(B, 1, D), sh.reshape(B, 1, D))


def _norm_shift_body(x_ref, g_ref, sc_ref, sh_ref, s0_ref, mix_ref, *rest, n_mix):
    outs = rest[:n_mix]
    last_ref = rest[n_mix]
    carry = rest[n_mix + 1]
    t = pl.program_id(1)

    @pl.when(t == 0)
    def _():
        carry[...] = s0_ref[0]

    h = _norm_mod_rows(x_ref[0], g_ref[...], sc_ref[0], sh_ref[0])
    tr = h.shape[0]
    if tr == 1:
        h_prev = carry[...]
    else:
        row = lax.broadcasted_iota(jnp.int32, h.shape, 0)
        h_prev = jnp.where(row == 0, carry[...], pltpu.roll(h, 1, 0))
    dx = h_prev - h
    mix = mix_ref[...]
    for i in range(n_mix):
        outs[i][0] = (h + dx * mix[i:i + 1]).astype(outs[i].dtype)
    last = h[tr - 1:tr, :]
    carry[...] = last
    last_ref[0] = last


def _norm_shift(x, g, sc, sh, shift_prev, mix):
    B, T, D = x.shape
    n_mix = mix.shape[0]
    tr = _pick_tile(T, (NORM_ROWS,))
    tok = pl.BlockSpec((1, tr, D), lambda b, t: (b, t, 0))
    per_b = pl.BlockSpec((1, 1, D), lambda b, t: (b, 0, 0))
    body = functools.partial(_norm_shift_body, n_mix=n_mix)
    outs = pl.pallas_call(
        body,
        grid=(B, T // tr),
        in_specs=[tok, pl.BlockSpec((1, D), lambda b, t: (0, 0)), per_b, per_b, per_b,
                  pl.BlockSpec((n_mix, D), lambda b, t: (0, 0))],
        out_specs=[tok] * n_mix + [per_b],
        out_shape=[jax.ShapeDtypeStruct((B, T, D), BF16)] * n_mix
        + [jax.ShapeDtypeStruct((B, 1, D), F32)],
        scratch_shapes=[pltpu.VMEM((1, D), F32)],
        compiler_params=_cparams(("parallel", "arbitrary")),
        name="norm_shift",
    )(x, g.reshape(1, D), sc.reshape(B, 1, D), sh.reshape(B, 1, D),
      shift_prev.astype(F32).reshape(B, 1, D), mix.astype(F32))
    return outs[:n_mix], outs[n_mix].reshape(B, D)


def _bdot(a, b):
    return lax.dot_general(a.astype(BF16), b.astype(BF16), (((2,), (1,)), ((0,), (0,))),
                           preferred_element_type=F32)


def _bdot_nt(a, b):
    return lax.dot_general(a.astype(BF16), b.astype(BF16), (((2,), (2,)), ((0,), (0,))),
                           preferred_element_type=F32)


def _bdot_tn(a, b):
    return lax.dot_general(a.astype(BF16), b.astype(BF16), (((1,), (1,)), ((0,), (0,))),
                           preferred_element_type=F32)


def _dot(a, b):
    return jnp.dot(a.astype(BF16), b.astype(BF16), preferred_element_type=F32)


def _dot_nt(a, b):
    return lax.dot_general(a.astype(BF16), b.astype(BF16), (((1,), (1,)), ((), ())),
                           preferred_element_type=F32)


def _head_sum(x, m0):
    s0 = jnp.sum(jnp.where(m0, x, 0.0), axis=-1, keepdims=True)
    s1 = jnp.sum(jnp.where(m0, 0.0, x), axis=-1, keepdims=True)
    return jnp.where(m0, s0, s1)


def _split_heads(x, m0):
    x0 = jnp.where(m0, x, 0.0)
    return jnp.concatenate([x0, x - x0], axis=-2)


def _rwkv_prep(r, k, v, wl, al, g, w0, a0, k_k, k_a, r_k):
    C = RW_CHUNK
    H = RW_HEAD
    P = r.shape[0]
    lane = lax.broadcasted_iota(jnp.int32, (1, 1, LANES), 2)
    m0 = lane < H

    log_decay = -DECAY_SCALE * _sigmoid(w0 + wl)
    a = _sigmoid(a0 + al)
    kk = k * k_k
    kk = kk * lax.rsqrt(jnp.maximum(_head_sum(kk * kk, m0), 1e-24))
    k2 = k * (1.0 + (a - 1.0) * k_a)
    b = kk * a

    ti = lax.broadcasted_iota(jnp.int32, (P, C, C), 1)
    si = lax.broadcasted_iota(jnp.int32, (P, C, C), 2)
    tri = jnp.where(ti >= si, 1.0, 0.0).astype(BF16)
    p_hi = log_decay.astype(BF16)
    rem = log_decay - p_hi.astype(F32)
    p_mid = rem.astype(BF16)
    p_lo = (rem - p_mid.astype(F32)).astype(BF16)
    cum3 = _bdot(tri, jnp.concatenate([p_hi, p_mid, p_lo], axis=2))
    cum = cum3[:, :, :LANES] + cum3[:, :, LANES:2 * LANES] + cum3[:, :, 2 * LANES:]

    mid = cum[:, C // 2 - 1:C // 2, :]
    last = cum[:, C - 1:C, :]
    e_fwd = jnp.exp(cum - mid)
    e_inv = jnp.exp(mid - cum)
    e_prev = jnp.exp(cum - log_decay - mid)
    d_mid = jnp.exp(mid)
    d_end_mid = jnp.exp(last - mid)
    d_end = jnp.exp(last)

    kk_t = kk * e_prev
    r_t = r * e_fwd
    b_t = b * e_inv
    k_t = k2 * e_inv
    bonus = _head_sum(r * k2 * r_k, m0) * v
    return kk_t, r_t, b_t, k_t, v, d_mid, d_end_mid, d_end, bonus, g


def _rwkv_chain(prep, lnx_w, lnx_b, S):
    kk_t, r_t, b_t, k_t, v, d_mid, d_end_mid, d_end, bonus, g = prep
    C = RW_CHUNK
    H = RW_HEAD
    lane = lax.broadcasted_iota(jnp.int32, (1, 1, LANES), 2)
    m0 = lane < H
    q = jnp.concatenate([kk_t, r_t], axis=1)
    q0 = jnp.where(m0, q, 0.0)
    g0 = _bdot_nt(q0, jnp.concatenate([b_t, k_t], axis=1))
    g1 = _bdot_nt(q - q0, jnp.concatenate([k_t, b_t], axis=1))

    t_row = lax.broadcasted_iota(jnp.int32, (1, C, LANES), 1)
    s_col = lax.broadcasted_iota(jnp.int32, (1, C, LANES), 2) & (C - 1)
    strict = t_row > s_col
    incl = t_row >= s_col
    g0t, g0b, g1t, g1b = g0[:, :C], g0[:, C:], g1[:, :C], g1[:, C:]
    l_cat = jnp.where(strict, jnp.where(m0, g0t, g1t), 0.0)
    ak_cat = jnp.where(strict, jnp.where(m0, g1t, g0t), 0.0)
    gb0 = jnp.where(incl, g0b, 0.0)
    gb1 = jnp.where(incl, g1b, 0.0)
    grb_cat = jnp.where(m0, gb0, gb1)

    rr = lax.broadcasted_iota(jnp.int32, (1, LANES, LANES), 1)
    cc = lax.broadcasted_iota(jnp.int32, (1, LANES, LANES), 2)
    bd_mask = (rr < H) == (cc < H)

    def block_diag(x_cat):
        return jnp.where(bd_mask, jnp.concatenate([x_cat, x_cat], axis=1), 0.0)

    t_cat = jnp.where(t_row == s_col, 1.0, 0.0) - l_cat
    m_cat = l_cat
    n = 1
    while 2 * n < C:
        m_cat = _bdot(m_cat, block_diag(m_cat))
        t_cat = t_cat + _bdot(t_cat, block_diag(m_cat))
        n *= 2

    v0 = jnp.where(m0, v, 0.0)
    v1 = v - v0
    kk_h = _bdot(t_cat, _split_heads(kk_t, m0))
    av = _bdot(ak_cat, jnp.concatenate([v1, v0], axis=1))
    u0 = -_bdot(t_cat, _split_heads(av, m0))
    r_h = r_t - _bdot(grb_cat, _split_heads(kk_h, m0))
    u00 = jnp.where(m0, u0, 0.0)
    y0 = _bdot(jnp.concatenate([gb0, gb1], axis=2),
               jnp.concatenate([u00, v0, v1, u0 - u00], axis=1))
    b_end = b_t * d_end_mid
    k_end = k_t * d_end_mid
    s_add = jnp.where(bd_mask, _bdot_tn(jnp.concatenate([u0, v], axis=1),
                                        jnp.concatenate([b_end, k_end], axis=1)), 0.0)
    kb = jnp.where(bd_mask, _bdot_tn(kk_h, b_end), 0.0)

    s_mid = S * d_mid
    y = _bdot_nt(r_h, s_mid) + y0
    s_new = S * d_end - _bdot(s_mid, kb) + s_add

    inv_h = 1.0 / H
    mu = _head_sum(y, m0) * inv_h
    yc = y - mu
    var = _head_sum(yc * yc, m0) * inv_h
    yn = yc * lax.rsqrt(var + GN_EPS) * lnx_w + lnx_b
    return (yn + bonus) * g, s_new


def _rwkv_chunk_body(r_ref, k_ref, v_ref, wl_ref, al_ref, g_ref,
                     w0_ref, a0_ref, kk_ref, ka_ref, rk_ref, lw_ref, lb_ref,
                     y_ref, s_out_ref, s_ref, *, n_pairs, n_sub, n_chunks):
    c = pl.program_id(2)

    @pl.when(c == 0)
    def _():
        s_ref[...] = jnp.zeros_like(s_ref)

    C = RW_CHUNK

    def pairs(ref, lead):
        return jnp.stack([ref[lead + (slice(p * LANES, (p + 1) * LANES),)]
                          for p in range(n_pairs)], axis=0)

    par = [pairs(ref, (slice(None),))
           for ref in (w0_ref, a0_ref, kk_ref, ka_ref, rk_ref, lw_ref, lb_ref)]

    def prep(j):
        rows = (0, slice(j * C, (j + 1) * C))
        tok = [pairs(ref, rows) for ref in (r_ref, k_ref, v_ref, wl_ref, al_ref, g_ref)]
        return _rwkv_prep(*tok, *par[:5])

    S = s_ref[...]
    nxt = prep(0)
    for j in range(n_sub):
        cur = nxt
        if j + 1 < n_sub:
            nxt = prep(j + 1)
        yg, S = _rwkv_chain(cur, par[5], par[6], S)
        for p in range(n_pairs):
            y_ref[0, j * C:(j + 1) * C, p * LANES:(p + 1) * LANES] = yg[p].astype(y_ref.dtype)
    s_ref[...] = S

    @pl.when(c == n_chunks - 1)
    def _():
        s_out_ref[0] = S


def _rwkv_prompt(r, k, v, wl, al, g, w0, a0, k_k, k_a, r_k, lnx_w, lnx_b):
    B, T, D = r.shape
    LW = RW_LANES_PER_STEP
    rows = RW_CHUNK * RW_SUBCHUNKS
    n_pairs = LW // LANES
    n_chunks = T // rows
    tok = pl.BlockSpec((1, rows, LW), lambda b, h, c: (b, c, h))
    par = pl.BlockSpec((1, LW), lambda b, h, c: (0, h))
    body = functools.partial(_rwkv_chunk_body, n_pairs=n_pairs, n_sub=RW_SUBCHUNKS,
                             n_chunks=n_chunks)
    y, s_bd = pl.pallas_call(
        body,
        grid=(B, D // LW, n_chunks),
        in_specs=[tok] * 6 + [par] * 7,
        out_specs=[pl.BlockSpec((1, rows, LW), lambda b, h, c: (b, c, h)),
                   pl.BlockSpec((1, n_pairs, LANES, LANES), lambda b, h, c: (b, h, 0, 0))],
        out_shape=[jax.ShapeDtypeStruct((B, T, D), BF16),
                   jax.ShapeDtypeStruct((B, D // LANES, LANES, LANES), F32)],
        scratch_shapes=[pltpu.VMEM((n_pairs, LANES, LANES), F32)],
        compiler_params=_cparams(("parallel", "parallel", "arbitrary")),
        name="rwkv_chunk",
    )(r, k, v, wl, al, g, *[p.reshape(1, D) for p in (w0, a0, k_k, k_a, r_k, lnx_w, lnx_b)])
    H = RW_HEAD
    s = jnp.stack([s_bd[:, :, :H, :H], s_bd[:, :, H:, H:]], axis=2)
    return y, s.reshape(B, D // H, H, H)


def _rwkv_step_body(s_ref, d_ref, kk_ref, b_ref, k_ref, r_ref, v_ref, s_out_ref, y_ref):
    S = s_ref[0]
    s_kk = jnp.sum(S * kk_ref[0], axis=-1, keepdims=True)
    S = S * d_ref[0] - s_kk * b_ref[0] + v_ref[0] * k_ref[0]
    s_out_ref[0] = S
    y_ref[0] = jnp.sum(S * r_ref[0], axis=-1)


def _rwkv_step(S0, decay, kk, b, k, r, v):
    B, Hh, N, _ = S0.shape
    row = lambda t: t.reshape(B, Hh, 1, N)
    row_spec = pl.BlockSpec((1, Hh, 1, N), lambda i: (i, 0, 0, 0))
    st_spec = pl.BlockSpec((1, Hh, N, N), lambda i: (i, 0, 0, 0))
    v_spread = jnp.broadcast_to(v.reshape(B, Hh, N, 1), (B, Hh, N, N))
    S1, y = pl.pallas_call(
        _rwkv_step_body,
        grid=(B,),
        in_specs=[st_spec] + [row_spec] * 5 + [st_spec],
        out_specs=[st_spec, pl.BlockSpec((1, Hh, N), lambda i: (i, 0, 0))],
        out_shape=[jax.ShapeDtypeStruct(S0.shape, F32), jax.ShapeDtypeStruct((B, Hh, N), F32)],
        compiler_params=_cparams(("parallel",)),
        name="rwkv_step",
    )(S0, row(decay), row(kk), row(b), row(k), row(r), v_spread)
    return S1, y.reshape(B, Hh * N)


def _swa_prompt_body(q_ref, kp_ref, kc_ref, vp_ref, vc_ref, qg_ref, sink_ref, o_ref):
    i = pl.program_id(1)
    W = WINDOW
    lane = lax.broadcasted_iota(jnp.int32, (1, LANES), 1)
    m0 = lane < HEAD_DIM
    a_idx = lax.broadcasted_iota(jnp.int32, (W, 2 * W), 0)
    c_idx = lax.broadcasted_iota(jnp.int32, (W, 2 * W), 1)
    first_key = jnp.where(i > 0, 0, W)
    mask = (c_idx > a_idx) & (c_idx <= a_idx + W) & (c_idx >= first_key)
    mask_all = jnp.concatenate([mask] * GQA_GROUP, axis=0)
    neg = jnp.float32(-jnp.inf)
    qg = qg_ref[...]

    for blk in range(N_KV_HEADS // 2):
        ksl = slice(blk * LANES, (blk + 1) * LANES)
        kb = jnp.concatenate([kp_ref[0, :, ksl], kc_ref[0, :, ksl]], axis=0)
        vb = jnp.concatenate([vp_ref[0, :, ksl], vc_ref[0, :, ksl]], axis=0)
        kb_r = pltpu.roll(kb, HEAD_DIM, 1)
        vb_r = pltpu.roll(vb, HEAD_DIM, 1)
        for sub in range(2):
            if sub == 0:
                k2 = jnp.where(m0, kb, kb_r)
                v2 = jnp.where(m0, vb, vb_r)
            else:
                k2 = jnp.where(m0, kb_r, kb)
                v2 = jnp.where(m0, vb_r, vb)
            k2 = k2.astype(BF16)
            v2 = v2.astype(BF16)
            kv = 2 * blk + sub
            n_pairs = GQA_GROUP // 2
            q_rows, sink_rows = [], []
            for pp in range(n_pairs):
                pair = kv * n_pairs + pp
                qsl = slice(pair * LANES, (pair + 1) * LANES)
                qv = q_ref[0, :, qsl]
                ms = _head_sum(qv * qv, m0) * (1.0 / HEAD_DIM)
                qn = qv * lax.rsqrt(ms + RMS_EPS) * qg * ATTN_SCALE
                q0 = jnp.where(m0, qn, 0.0)
                q_rows += [q0, qn - q0]
                sv = sink_ref[:, qsl]
                for hm in (m0, jnp.logical_not(m0)):
                    sink = jnp.max(jnp.where(hm, sv, neg), axis=-1, keepdims=True)
                    sink_rows.append(jnp.broadcast_to(sink, (W, LANES)))
            q_all = jnp.concatenate(q_rows, axis=0).astype(BF16)
            sink = jnp.concatenate(sink_rows, axis=0)
            s = lax.dot_general(q_all, k2, (((1,), (1,)), ((), ())), preferred_element_type=F32)
            s = jnp.where(mask_all, s, neg)
            s_a, s_b = s[:, :LANES], s[:, LANES:]
            row_max = jnp.max(jnp.maximum(s_a, s_b), axis=-1, keepdims=True)
            m = jnp.maximum(row_max, sink)
            p_a = jnp.exp(s_a - m)
            p_b = jnp.exp(s_b - m)
            row_sum = jnp.sum(p_a + p_b, axis=-1, keepdims=True)
            inv = 1.0 / (row_sum + jnp.exp(sink - m))
            p = jnp.concatenate([p_a * inv, p_b * inv], axis=1)
            o = jnp.dot(p.astype(BF16), v2, preferred_element_type=F32)
            for pp in range(n_pairs):
                pair = kv * n_pairs + pp
                o0 = o[(2 * pp) * W:(2 * pp + 1) * W]
                o1 = o[(2 * pp + 1) * W:(2 * pp + 2) * W]
                o_ref[0, :, pair * LANES:(pair + 1) * LANES] = jnp.where(m0, o0, o1).astype(o_ref.dtype)


def _swa_prompt(q, k, v, q_norm_g, sinks):
    B, T, D = q.shape
    KVW = k.shape[-1]
    W = WINDOW
    qg = jnp.tile(q_norm_g.astype(F32), 2).reshape(1, LANES)
    sink_l = jnp.repeat(sinks.astype(F32), HEAD_DIM).reshape(1, D)
    cur = pl.BlockSpec((1, W, KVW), lambda b, i: (b, i, 0))
    prev = pl.BlockSpec((1, W, KVW), lambda b, i: (b, jnp.maximum(i - 1, 0), 0))
    return pl.pallas_call(
        _swa_prompt_body,
        grid=(B, T // W),
        in_specs=[pl.BlockSpec((1, W, D), lambda b, i: (b, i, 0)), prev, cur, prev, cur,
                  pl.BlockSpec((1, LANES), lambda b, i: (0, 0)),
                  pl.BlockSpec((1, D), lambda b, i: (0, 0))],
        out_specs=pl.BlockSpec((1, W, D), lambda b, i: (b, i, 0)),
        out_shape=jax.ShapeDtypeStruct((B, T, D), BF16),
        compiler_params=_cparams(("parallel", "parallel")),
        name="swa_prompt",
    )(q, k, k, v, v, qg, sink_l)


def _swa_sample_body(q_ref, kc_ref, vc_ref, kn_ref, vn_ref, qg_ref, sink_ref, o_ref):
    G, HD = GQA_GROUP, HEAD_DIM
    q = q_ref[0]
    ms = jnp.mean(q * q, axis=-1, keepdims=True)
    qn = q * lax.rsqrt(ms + RMS_EPS) * qg_ref[...] * ATTN_SCALE
    n_buf = kc_ref.shape[1]
    col = lax.broadcasted_iota(jnp.int32, (G, n_buf), 1)
    valid = (n_buf - col) < WINDOW
    neg = jnp.float32(-jnp.inf)
    KV = N_KV_HEADS

    def per_kv(ref):
        return jnp.stack([ref[0, :, kv * HD:(kv + 1) * HD] for kv in range(KV)], axis=0)

    q3 = jnp.stack([qn[kv * G:(kv + 1) * G, :] for kv in range(KV)], axis=0)
    kc, vc, kn, vn = per_kv(kc_ref), per_kv(vc_ref), per_kv(kn_ref), per_kv(vn_ref)
    sink = jnp.stack([sink_ref[kv * G:(kv + 1) * G, :] for kv in range(KV)], axis=0)
    s_c = jnp.where(valid, _bdot_nt(q3, kc), neg)
    s_n = jnp.sum(q3.astype(BF16).astype(F32) * kn.astype(BF16).astype(F32),
                  axis=-1, keepdims=True)
    m = jnp.maximum(jnp.maximum(jnp.max(s_c, axis=-1, keepdims=True), s_n), sink)
    p_c = jnp.exp(s_c - m)
    p_n = jnp.exp(s_n - m)
    denom = jnp.sum(p_c, axis=-1, keepdims=True) + p_n + jnp.exp(sink - m)
    p_c = p_c / denom
    p_n = p_n / denom
    o = _bdot(p_c, vc) + p_n.astype(BF16).astype(F32) * vn.astype(BF16).astype(F32)
    for kv in range(KV):
        o_ref[0, kv * G:(kv + 1) * G, :] = o[kv].astype(o_ref.dtype)


def _swa_sample(q, k_buf, v_buf, k_new, v_new, q_norm_g, sinks):
    B, D = q.shape
    n_buf, KVW = k_buf.shape[1:]
    NH, HD = N_Q_HEADS, HEAD_DIM
    buf = pl.BlockSpec((1, n_buf, KVW), lambda b: (b, 0, 0))
    new = pl.BlockSpec((1, 1, KVW), lambda b: (b, 0, 0))
    o = pl.pallas_call(
        _swa_sample_body,
        grid=(B,),
        in_specs=[pl.BlockSpec((1, NH, HD), lambda b: (b, 0, 0)), buf, buf, new, new,
                  pl.BlockSpec((1, HD), lambda b: (0, 0)),
                  pl.BlockSpec((NH, 1), lambda b: (0, 0))],
        out_specs=pl.BlockSpec((1, NH, HD), lambda b: (b, 0, 0)),
        out_shape=jax.ShapeDtypeStruct((B, NH, HD), BF16),
        compiler_params=_cparams(("parallel",)),
        name="swa_sample",
    )(q.reshape(B, NH, HD), k_buf, v_buf, k_new, v_new,
      q_norm_g.astype(F32).reshape(1, HD), sinks.astype(F32).reshape(NH, 1))
    return o.reshape(B, D)


def _rms(x, g):
    return x * lax.rsqrt(jnp.mean(x * x, axis=-1, keepdims=True) + RMS_EPS) * g


def _silu(x):
    return x * jax.nn.sigmoid(x)


def _forward(x, mods, kvmod, wkv0, shift0, conv0, k_buf, v_buf, p, wsrc, tag):
    B, T, D = x.shape
    M = B * T
    depth = len(mods)
    n_a = depth // 2
    emit = T == 1
    wb = {}

    def mm(xin, wname, l=None, **kw):
        if emit:
            out, w16 = _mm(xin, wsrc[wname], layer=l, emit_w=True, name=f"{tag}_{wname}", **kw)
            wb[(wname, l)] = w16
            return out
        w = wsrc[wname] if l is None else wsrc[wname][l]
        return _mm(xin, w, name=f"{tag}_{wname}", **kw)

    new_wkv, new_shift, new_conv = [], [], []
    k_att = v_att = k_win = v_win = None
    x2 = x.reshape(M, D)
    for l in range(depth):
        sh1, sc1, gt1, sh2, sc2, gt2 = jnp.split(mods[l], 6, axis=-1)
        x3 = x2.reshape(B, T, D)
        if l < n_a:
            mixes, h_last = _norm_shift(x3, p['ln1_g'][l], sc1, sh1, shift0[l], p['rwkv_mix'][l])
            xr, xw, xk, xv, xa, xg = [m.reshape(M, D) for m in mixes]
            r = mm(xr, 'rwkv_w_r', l)
            k = mm(xk, 'rwkv_w_k', l)
            v = mm(xv, 'rwkv_w_v', l)
            wl = mm(mm(xw, 'rwkv_w1', l, act="tanh", out_dtype=BF16), 'rwkv_w2', l)
            al = mm(mm(xa, 'rwkv_a1', l, out_dtype=BF16), 'rwkv_a2', l)
            g = mm(mm(xg, 'rwkv_g1', l, act="sigmoid", out_dtype=BF16), 'rwkv_g2', l)
            w0, a0 = p['rwkv_w0'][l], p['rwkv_a0'][l]
            k_k, k_a = p['rwkv_k_k'][l], p['rwkv_k_a'][l]
            r_k = p['rwkv_r_k'][l].reshape(D)
            lnx_w, lnx_b = p['rwkv_lnx_w'][l], p['rwkv_lnx_b'][l]
            if T > 1:
                sh = (B, T, D)
                yg, S = _rwkv_prompt(r.reshape(sh), k.reshape(sh), v.reshape(sh), wl.reshape(sh),
                                     al.reshape(sh), g.reshape(sh), w0, a0, k_k, k_a, r_k,
                                     lnx_w, lnx_b)
                yg = yg.reshape(M, D)
            else:
                Hh, N = RW_HEADS, RW_HEAD
                w_log = -jax.nn.softplus(-(w0 + wl)) - 0.5
                decay = jnp.exp(-jnp.exp(w_log))
                a = jax.nn.sigmoid(a0 + al)
                kk = (k * k_k).reshape(B, Hh, N)
                kk = kk / jnp.maximum(jnp.sqrt(jnp.sum(kk * kk, axis=-1, keepdims=True)), 1e-12)
                kk = kk.reshape(B, D)
                k2 = k * (1.0 + (a - 1.0) * k_a)
                S, y = _rwkv_step(wkv0[l].astype(F32), decay, kk, kk * a, k2, r, v)
                y4 = y.reshape(B, Hh, N)
                mu = jnp.mean(y4, axis=-1, keepdims=True)
                var = jnp.mean(jnp.square(y4 - mu), axis=-1, keepdims=True)
                y4 = ((y4 - mu) * lax.rsqrt(var + GN_EPS) * lnx_w.reshape(Hh, N)
                      + lnx_b.reshape(Hh, N))
                bonus = jnp.sum((r * k2 * r_k).reshape(B, Hh, N), axis=-1, keepdims=True)
                y4 = y4 + bonus * v.reshape(B, Hh, N)
                yg = (y4.reshape(B, D) * g).astype(BF16)
            new_wkv.append(S)
            new_shift.append(h_last)
            x2 = mm(yg, 'rwkv_w_o', l, res=x2, gate=gt1, rows_per_gate=T)
        else:
            j = l - n_a
            h = _norm_mod(x3, p['ln1_g'][l], sc1, sh1).reshape(M, D)
            q = mm(h, 'attn_w_q', j)
            if k_buf is None:
                attn = _swa_prompt(q.reshape(B, T, D), k_att, v_att, p['attn_q_norm_g'][j],
                                   p['attn_sinks'][j]).reshape(M, D)
            else:
                attn = _swa_sample(q, k_buf, v_buf, k_att, v_att, p['attn_q_norm_g'][j],
                                   p['attn_sinks'][j])
            x2 = mm(attn, 'attn_w_o', j, res=x2, gate=gt1, rows_per_gate=T)
        h2 = _norm_mod(x2.reshape(B, T, D), p['ln2_g'][l], sc2, sh2).reshape(M, D)
        if emit:
            act, cb, wg16, wu16 = _ffn_in_tok(h2, wsrc['ffn_w_in'], l, p['ffn_conv_w'][l],
                                              p['ffn_conv_b'][l], conv0[l])
            wb[('ffn_w_gate', l)] = wg16
            wb[('ffn_w_up', l)] = wu16
        else:
            act, cb = _ffn_in_seq(h2, wsrc['ffn_w_gate'][l], wsrc['ffn_w_up'][l],
                                  p['ffn_conv_w'][l], p['ffn_conv_b'][l], conv0[l], T)
        new_conv.append(cb)
        x2 = mm(act, 'ffn_w_out', l, res=x2, gate=gt2, rows_per_gate=T)
        if l == n_a - 1:
            sh, sc = jnp.split(kvmod, 2, axis=-1)
            hn = _norm_mod(x2.reshape(B, T, D), p['kv_norm_g'], sc, sh).reshape(M, D)
            kv = mm(hn, 'w_kv')
            KVW = kv.shape[-1] // 2
            k_new = _rms(kv[:, :KVW].reshape(B, T, N_KV_HEADS, HEAD_DIM), p['k_norm_g'])
            v_new = kv[:, KVW:].reshape(B, T, N_KV_HEADS, HEAD_DIM)
            k_att, v_att = k_new.reshape(B, T, KVW), v_new.reshape(B, T, KVW)
            if k_buf is None:
                w = min(WINDOW, T)
                k_win, v_win = k_new[:, T - w:], v_new[:, T - w:]
            else:
                n_buf = k_buf.shape[1]
                k_win = jnp.concatenate([k_buf, k_new], axis=1)[:, -n_buf:]
                v_win = jnp.concatenate([v_buf, v_new], axis=1)[:, -n_buf:]
                k_buf = k_buf.reshape(B, n_buf, KVW)
                v_buf = v_buf.reshape(B, n_buf, KVW)
    outs = (x2.reshape(B, T, D), jnp.stack(new_wkv), jnp.stack(new_shift), jnp.stack(new_conv),
            k_win, v_win)
    return outs, wb


class _Layered:
    def __init__(self, wb):
        self._wb = wb

    def __getitem__(self, name):
        if (name, None) in self._wb:
            return self._wb[(name, None)]
        n = 1 + max(l for (nm, l) in self._wb if nm == name)
        return [self._wb[(name, l)] for l in range(n)]


def kernel(x_prompt, x_sample, c_prompt, c_sample, state_wkv, state_shift, state_conv, cache_k_win, cache_v_win, mod_w, mod_b, ln1_g, ln2_g, rwkv_mix, rwkv_w0, rwkv_w1, rwkv_w2, rwkv_a0, rwkv_a1, rwkv_a2, rwkv_g1, rwkv_g2, rwkv_k_k, rwkv_k_a, rwkv_r_k, rwkv_w_r, rwkv_w_k, rwkv_w_v, rwkv_w_o, rwkv_lnx_w, rwkv_lnx_b, kv_norm_g, kv_mod_w, kv_mod_b, w_kv, k_norm_g, attn_w_q, attn_q_norm_g, attn_sinks, attn_w_o, ffn_w_in, ffn_conv_w, ffn_conv_b, ffn_w_out):
    p = dict(mod_w=mod_w, mod_b=mod_b, ln1_g=ln1_g, ln2_g=ln2_g,
             rwkv_mix=rwkv_mix, rwkv_w0=rwkv_w0, rwkv_w1=rwkv_w1, rwkv_w2=rwkv_w2,
             rwkv_a0=rwkv_a0, rwkv_a1=rwkv_a1, rwkv_a2=rwkv_a2, rwkv_g1=rwkv_g1, rwkv_g2=rwkv_g2,
             rwkv_k_k=rwkv_k_k, rwkv_k_a=rwkv_k_a, rwkv_r_k=rwkv_r_k, rwkv_w_r=rwkv_w_r,
             rwkv_w_k=rwkv_w_k, rwkv_w_v=rwkv_w_v, rwkv_w_o=rwkv_w_o,
             rwkv_lnx_w=rwkv_lnx_w, rwkv_lnx_b=rwkv_lnx_b,
             kv_norm_g=kv_norm_g, kv_mod_w=kv_mod_w, kv_mod_b=kv_mod_b, w_kv=w_kv, k_norm_g=k_norm_g,
             attn_w_q=attn_w_q, attn_q_norm_g=attn_q_norm_g, attn_sinks=attn_sinks, attn_w_o=attn_w_o,
             ffn_w_in=ffn_w_in, ffn_conv_w=ffn_conv_w, ffn_conv_b=ffn_conv_b, ffn_w_out=ffn_w_out)
    depth = mod_w.shape[0]
    n_a = depth // 2
    Bp = x_prompt.shape[0]
    dt = x_prompt.dtype

    c_all = _silu(jnp.concatenate([c_prompt, c_sample], axis=0)).astype(BF16)
    mods = [_mm(c_all, mod_w, layer=l, bias=mod_b[l], name=f"mod{l}") for l in range(depth)]
    kvmod = _mm(c_all, kv_mod_w, bias=kv_mod_b, name="kvmod")

    outs_s, wb = _forward(x_sample, [m[Bp:] for m in mods], kvmod[Bp:], state_wkv, state_shift,
                          state_conv, cache_k_win, cache_v_win, p, p, "s")
    F = ffn_conv_b.shape[-1]
    wkv0 = jnp.zeros((n_a, Bp, RW_HEADS, RW_HEAD, RW_HEAD), dt)
    shift0 = jnp.zeros((n_a, Bp, D_MODEL), dt)
    conv0 = jnp.zeros((depth, Bp, CONV_W - 1, F), dt)
    outs_p, _ = _forward(x_prompt, [m[:Bp] for m in mods], kvmod[:Bp], wkv0, shift0, conv0,
                         None, None, p, _Layered(wb), "p")
    y_p, wkv_p, shift_p, conv_p, kwin_p, vwin_p = outs_p
    y_s, wkv_s, shift_s, conv_s, kwin_s, vwin_s = outs_s
    return (y_p, y_s, wkv_p, wkv_s, shift_p, shift_s, conv_p, conv_s,
            kwin_p, kwin_s, vwin_p, vwin_s)
```
